```python
import math
import jax, jax.numpy as jnp
from jax import lax
import numpy as np

D_MODEL = 1024
BATCH = 2
SEQ = 8192
DEPTH = 2

FOX_HEADS = 8
FOX_HEAD_DIM = 64
FOX_WIDTH = FOX_HEADS * FOX_HEAD_DIM
Q_BLOCK = 128
CONV_CHANNELS = D_MODEL // 2
CONV_WIDTH = 31
MLSTM_HEADS = 8
MLSTM_HEAD_DIM = 128
MLSTM_WIDTH = MLSTM_HEADS * MLSTM_HEAD_DIM
MLSTM_CONV_WIDTH = 4
MLSTM_CHUNK = 64
D_FF = 2816
N_EXPERTS = 8
TOP_K = 2
D_FF_EXPERT = 3584
EPS = 1e-6

IN0_WIDTH = 3 * FOX_WIDTH + FOX_HEADS + 2 * CONV_CHANNELS
IN1_WIDTH = 3 * MLSTM_WIDTH + 2 * MLSTM_HEADS

kernel_name = "fox_conformer_mlstm_moe_hybrid"


def rms_norm(x, g):
    xf = x.astype(jnp.float32)
    y = xf * lax.rsqrt(jnp.mean(xf * xf, axis=-1, keepdims=True) + EPS)
    return (y * g.astype(jnp.float32)).astype(x.dtype)


def layer_norm(x, g, b):
    xf = x.astype(jnp.float32)
    mu = jnp.mean(xf, axis=-1, keepdims=True)
    var = jnp.mean(jnp.square(xf - mu), axis=-1, keepdims=True)
    y = (xf - mu) * lax.rsqrt(var + EPS)
    return (y * g.astype(jnp.float32) + b.astype(jnp.float32)).astype(x.dtype)


def causal_depthwise_conv(x, w, b):
    width, ch = w.shape
    y = lax.conv_general_dilated(
        x, w[:, None, :].astype(x.dtype), window_strides=(1,), padding=[(width - 1, 0)],
        dimension_numbers=("NWC", "WIO", "NWC"), feature_group_count=ch)
    return y + b.astype(x.dtype)


def swiglu(h, w_gate, w_up, w_down):
    return (jax.nn.silu(h @ w_gate) * (h @ w_up)) @ w_down


def forgetting_attention(q, k, v, log_f):
    b_, h_, s_, dh = q.shape
    nb = s_ // Q_BLOCK
    c = jnp.cumsum(log_f, axis=-1)
    scale = dh ** -0.5
    qb = q.reshape(b_, h_, nb, Q_BLOCK, dh).transpose(2, 0, 1, 3, 4)
    cb = c.reshape(b_, h_, nb, Q_BLOCK).transpose(2, 0, 1, 3)
    key_pos = jnp.arange(s_)

    def one_block(args):
        q_blk, c_blk, blk = args
        logits = jnp.einsum("bhqd,bhkd->bhqk", q_blk, k,
                            preferred_element_type=jnp.float32) * scale
        logits = logits + (c_blk[..., :, None] - c[..., None, :])
        q_pos = blk * Q_BLOCK + jnp.arange(Q_BLOCK)
        logits = jnp.where(key_pos[None, :] <= q_pos[:, None], logits, -jnp.inf)
        p = jax.nn.softmax(logits, axis=-1)
        return jnp.einsum("bhqk,bhkd->bhqd", p.astype(v.dtype), v)

    out = lax.map(one_block, (qb, cb, jnp.arange(nb)))
    return out.transpose(1, 2, 0, 3, 4).reshape(b_, h_, s_, dh)


def mlstm_chunkwise(q, k, v, log_i, log_f):
    b_, h_, s_, dk = q.shape
    dv = v.shape[-1]
    L = MLSTM_CHUNK
    nc = s_ // L

    def chunks(t):
        return jnp.moveaxis(t.reshape(b_, h_, nc, L, *t.shape[3:]), 2, 0)

    qc = chunks(q.astype(jnp.float32))
    kc = chunks(k.astype(jnp.float32))
    vc = chunks(v.astype(jnp.float32))
    ic, fc = chunks(log_i), chunks(log_f)
    bcum = jnp.cumsum(fc, axis=-1)
    gtot = bcum[..., -1]
    a_end = gtot[..., None] - bcum + ic
    tri = jnp.tril(jnp.ones((L, L), dtype=bool))
    dmat = jnp.where(tri, bcum[..., :, None] - bcum[..., None, :] + ic[..., None, :], -jnp.inf)

    def step(carry, xs):
        C, n, m = carry
        q_, k_, v_, b_c, g_c, a_c, d_c = xs
        inter = b_c + m[..., None]
        m_t = jnp.maximum(inter, jnp.max(d_c, axis=-1))
        w_inter = jnp.exp(inter - m_t)
        w_intra = jnp.exp(d_c - m_t[..., None]) * jnp.einsum("bhld,bhsd->bhls", q_, k_)
        num = (w_inter[..., None] * jnp.einsum("bhvd,bhld->bhlv", C, q_)
               + jnp.einsum("bhls,bhsv->bhlv", w_intra, v_))
        nq = w_inter * jnp.einsum("bhd,bhld->bhl", n, q_) + jnp.sum(w_intra, axis=-1)
        h = num / jnp.maximum(jnp.abs(nq), jnp.exp(-m_t))[..., None]
        m_new = jnp.maximum(g_c + m, jnp.max(a_c, axis=-1))
        decay = jnp.exp(g_c + m - m_new)
        w_in = jnp.exp(a_c - m_new[..., None])
        C_new = decay[..., None, None] * C + jnp.einsum("bhs,bhsv,bhsd->bhvd", w_in, v_, k_)
        n_new = decay[..., None] * n + jnp.einsum("bhs,bhsd->bhd", w_in, k_)
        return (C_new, n_new, m_new), h

    init = (jnp.zeros((b_, h_, dv, dk), jnp.float32),
            jnp.zeros((b_, h_, dk), jnp.float32),
            jnp.zeros((b_, h_), jnp.float32))
    _, hs = lax.scan(step, init, (qc, kc, vc, bcum, gtot, a_end, dmat))
    return jnp.moveaxis(hs, 0, 2).reshape(b_, h_, s_, dv)


def even_mixer(h, w_in, b_f, dw_w, dw_b, cln_g, cln_b, w_out):
    b_, s_, _ = h.shape
    proj = h @ w_in
    q, k, v, f_pre, glu_in = jnp.split(
        proj, [FOX_WIDTH, 2 * FOX_WIDTH, 3 * FOX_WIDTH, 3 * FOX_WIDTH + FOX_HEADS], axis=-1)

    def heads(t):
        return t.reshape(b_, s_, FOX_HEADS, FOX_HEAD_DIM).transpose(0, 2, 1, 3)

    log_f = jax.nn.log_sigmoid((f_pre + b_f).astype(jnp.float32)).transpose(0, 2, 1)
    att = forgetting_attention(heads(q), heads(k), heads(v), log_f)
    att = att.transpose(0, 2, 1, 3).reshape(b_, s_, FOX_WIDTH)

    a, gate = jnp.split(glu_in, 2, axis=-1)
    u = a * jax.nn.sigmoid(gate)
    u = causal_depthwise_conv(u, dw_w, dw_b)
    u = jax.nn.silu(layer_norm(u, cln_g, cln_b))

    return jnp.concatenate([att, u], axis=-1) @ w_out


def odd_mixer(h, w_in, b_i, b_f, conv_w, conv_b, wq_head, wk_head, hnorm_g, w_out):
    b_, s_, _ = h.shape
    proj = h @ w_in
    u, v, o, i_pre, f_pre = jnp.split(
        proj, [MLSTM_WIDTH, 2 * MLSTM_WIDTH, 3 * MLSTM_WIDTH, 3 * MLSTM_WIDTH + MLSTM_HEADS], axis=-1)
    u = jax.nn.silu(causal_depthwise_conv(u, conv_w, conv_b))
    u = u.reshape(b_, s_, MLSTM_HEADS, MLSTM_HEAD_DIM)
    q = jnp.einsum("bshd,hde->bhse", u, wq_head)
    k = jnp.einsum("bshd,hde->bhse", u, wk_head) * (MLSTM_HEAD_DIM ** -0.5)
    vh = v.reshape(b_, s_, MLSTM_HEADS, MLSTM_HEAD_DIM).transpose(0, 2, 1, 3)
    log_i = (i_pre + b_i).astype(jnp.float32).transpose(0, 2, 1)
    log_f = jax.nn.log_sigmoid((f_pre + b_f).astype(jnp.float32)).transpose(0, 2, 1)
    h_tilde = mlstm_chunkwise(q, k, vh, log_i, log_f)
    h_tilde = h_tilde.transpose(0, 2, 1, 3).astype(h.dtype)
    og = jax.nn.sigmoid(o).reshape(b_, s_, MLSTM_HEADS, MLSTM_HEAD_DIM)
    hc = og * h_tilde
    hc = rms_norm(hc, hnorm_g.reshape(MLSTM_HEADS, MLSTM_HEAD_DIM))
    return hc.reshape(b_, s_, MLSTM_WIDTH) @ w_out


def moe_swiglu(h, router, e_gate, e_up, e_down):
    logits = (h @ router).astype(jnp.float32)
    top_vals, top_idx = lax.top_k(logits, TOP_K)
    gates = jax.nn.softmax(top_vals, axis=-1)
    combine = jnp.sum(jax.nn.one_hot(top_idx, N_EXPERTS, dtype=jnp.float32) * gates[..., None],
                      axis=-2).astype(h.dtype)
    y = jnp.zeros_like(h)
    for e in range(N_EXPERTS):
        y = y + combine[..., e:e + 1] * swiglu(h, e_gate[e], e_up[e], e_down[e])
    return y


def setup_inputs(seed: int = 0) -> dict:
    key = jax.random.key(seed)
    ks = iter(jax.random.split(key, 32))

    def nrm(shape, scale):
        return jax.random.normal(next(ks), shape, jnp.float32) * scale

    def gain(n):
        return 1.0 + nrm((n,), 0.02)

    D = D_MODEL
    return {
        "x": nrm((BATCH, SEQ, D), 1.0),
        "l0_norm1_g": gain(D),
        "l0_w_in": nrm((D, IN0_WIDTH), D ** -0.5),
        "l0_b_f": 2.0 + nrm((FOX_HEADS,), 0.5),
        "l0_dw_w": nrm((CONV_WIDTH, CONV_CHANNELS), CONV_WIDTH ** -0.5),
        "l0_dw_b": nrm((CONV_CHANNELS,), 0.02),
        "l0_cln_g": gain(CONV_CHANNELS),
        "l0_cln_b": nrm((CONV_CHANNELS,), 0.02),
        "l0_w_out": nrm((FOX_WIDTH + CONV_CHANNELS, D), (FOX_WIDTH + CONV_CHANNELS) ** -0.5),
        "l0_norm2_g": gain(D),
        "l0_w_gate": nrm((D, D_FF), D ** -0.5),
        "l0_w_up": nrm((D, D_FF), D ** -0.5),
        "l0_w_down": nrm((D_FF, D), D_FF ** -0.5),
        "l1_norm1_g": gain(D),
        "l1_w_in": nrm((D, IN1_WIDTH), D ** -0.5),
        "l1_b_i": nrm((MLSTM_HEADS,), 0.1),
        "l1_b_f": 3.0 + nrm((MLSTM_HEADS,), 0.5),
        "l1_conv_w": nrm((MLSTM_CONV_WIDTH, MLSTM_WIDTH), MLSTM_CONV_WIDTH ** -0.5),
        "l1_conv_b": nrm((MLSTM_WIDTH,), 0.02),
        "l1_wq_head": nrm((MLSTM_HEADS, MLSTM_HEAD_DIM, MLSTM_HEAD_DIM), MLSTM_HEAD_DIM ** -0.5),
        "l1_wk_head": nrm((MLSTM_HEADS, MLSTM_HEAD_DIM, MLSTM_HEAD_DIM), MLSTM_HEAD_DIM ** -0.5),
        "l1_hnorm_g": gain(MLSTM_WIDTH),
        "l1_w_out": nrm((MLSTM_WIDTH, D), MLSTM_WIDTH ** -0.5),
        "l1_norm2_g": gain(D),
        "l1_router": nrm((D, N_EXPERTS), D ** -0.5),
        "l1_e_gate": nrm((N_EXPERTS, D, D_FF_EXPERT), D ** -0.5),
        "l1_e_up": nrm((N_EXPERTS, D, D_FF_EXPERT), D ** -0.5),
        "l1_e_down": nrm((N_EXPERTS, D_FF_EXPERT, D), D_FF_EXPERT ** -0.5),
        "final_norm_g": gain(D),
    }


def reference(x, l0_norm1_g, l0_w_in, l0_b_f, l0_dw_w, l0_dw_b, l0_cln_g, l0_cln_b, l0_w_out,
              l0_norm2_g, l0_w_gate, l0_w_up, l0_w_down,
              l1_norm1_g, l1_w_in, l1_b_i, l1_b_f, l1_conv_w, l1_conv_b, l1_wq_head, l1_wk_head,
              l1_hnorm_g, l1_w_out, l1_norm2_g, l1_router, l1_e_gate, l1_e_up, l1_e_down,
              final_norm_g):
    for layer in range(DEPTH):
        if layer % 2 == 0:
            x = x + even_mixer(rms_norm(x, l0_norm1_g), l0_w_in, l0_b_f, l0_dw_w, l0_dw_b,
                               l0_cln_g, l0_cln_b, l0_w_out)
            x = x + swiglu(rms_norm(x, l0_norm2_g), l0_w_gate, l0_w_up, l0_w_down)
        else:
            x = x + odd_mixer(rms_norm(x, l1_norm1_g), l1_w_in, l1_b_i, l1_b_f, l1_conv_w,
                              l1_conv_b, l1_wq_head, l1_wk_head, l1_hnorm_g, l1_w_out)
            x = x + moe_swiglu(rms_norm(x, l1_norm2_g), l1_router, l1_e_gate, l1_e_up, l1_e_down)
    return rms_norm(x, final_norm_g)
```

```python
import functools

import jax
import jax.numpy as jnp
from jax import lax
from jax.experimental import pallas as pl
from jax.experimental.pallas import tpu as pltpu

F32 = jnp.float32
BF16 = jnp.bfloat16
I32 = jnp.int32

EPS = 1e-6
NEG = -1e30

D_MODEL = 1024
FOX_HEADS = 8
FOX_HEAD_DIM = 64
FOX_WIDTH = FOX_HEADS * FOX_HEAD_DIM
CONV_CHANNELS = 512
CONV_WIDTH = 31
MLSTM_HEADS = 8
MLSTM_HEAD_DIM = 128
MLSTM_WIDTH = MLSTM_HEADS * MLSTM_HEAD_DIM
MLSTM_CONV_WIDTH = 4
D_FF = 2816
N_EXPERTS = 8
D_FF_EXPERT = 3584

LANES = 128
SUBLANES = 8
MIB = 1024 * 1024

TM_IN0 = 512
TQ = 512
TM_CONV = 512
CONV_ROWS = 64
CONV_HALO = 32
TM_FFN0 = 512
TM_IN1 = 512
L1_HALO = 8
CHUNK = 256
TM_OUT1 = 512
TM_E = 512
TF_E = 512
TM_FIN = 512


def _cparams(sem, vmem_mib):
    return pltpu.CompilerParams(dimension_semantics=sem, vmem_limit_bytes=vmem_mib * MIB)


def _rms(x, g):
    return x * lax.rsqrt(jnp.mean(x * x, axis=-1, keepdims=True) + EPS) * g


def _sigmoid(x):
    return 1.0 / (1.0 + jnp.exp(-x))


def _log_sigmoid(x):
    return jnp.minimum(x, 0.0) - jnp.log(1.0 + jnp.exp(-jnp.abs(x)))


def _dot(a, b):
    return jnp.dot(a, b, preferred_element_type=F32)


def _dot_nt(a, b):
    return lax.dot_general(a, b, (((1,), (1,)), ((), ())), preferred_element_type=F32)


def _split3(x):
    hi = x.astype(BF16)
    r1 = x - hi.astype(F32)
    mid = r1.astype(BF16)
    lo = (r1 - mid.astype(F32)).astype(BF16)
    return hi, mid, lo


def _const_spec(shape):
    nd = len(shape)
    return pl.BlockSpec(shape, lambda *_: (0,) * nd, pipeline_mode=pl.Buffered(1))


def _l0_in_kernel(x_ref, g_ref, wqkv_ref, wft_ref, bf_ref, wa_ref, wg_ref,
                  q_ref, k_ref, v_ref, c_ref, u_ref, carry_ref, *, tiles_per_batch):
    i = pl.program_id(0)
    tm = x_ref.shape[0]
    hb = _rms(x_ref[...], g_ref[...]).astype(BF16)
    qkv = _dot(hb, wqkv_ref[...])
    q_ref[...] = (qkv[:, :FOX_WIDTH] * (FOX_HEAD_DIM ** -0.5)).astype(BF16)
    k_ref[...] = qkv[:, FOX_WIDTH:2 * FOX_WIDTH].astype(BF16)
    v_ref[...] = qkv[:, 2 * FOX_WIDTH:].astype(BF16)
    u_ref[...] = _dot(hb, wa_ref[...]) * _sigmoid(_dot(hb, wg_ref[...]))

    logf = _log_sigmoid(_dot_nt(wft_ref[...], hb) + bf_ref[...])
    lane = lax.broadcasted_iota(I32, logf.shape, 1)
    cum = logf
    shift = 1
    while shift < tm:
        cum = cum + jnp.where(lane >= shift, pltpu.roll(cum, shift, 1), 0.0)
        shift *= 2

    @pl.when(i % tiles_per_batch == 0)
    def _():
        carry_ref[...] = jnp.zeros_like(carry_ref)

    c = cum + carry_ref[:, 0:1]
    c_ref[...] = c
    carry_ref[...] = jnp.broadcast_to(c[:, tm - 1:tm], carry_ref.shape)


def _l0_in(x, g, wqkv, wft, bf, wa, wg, *, batch):
    t = x.shape[0]
    seq = t // batch
    tm = TM_IN0
    tpb = seq // tm
    row = lambda w: pl.BlockSpec((tm, w), lambda i: (i, 0))
    return pl.pallas_call(
        functools.partial(_l0_in_kernel, tiles_per_batch=tpb),
        grid=(t // tm,),
        in_specs=[row(D_MODEL), _const_spec(g.shape), _const_spec(wqkv.shape), _const_spec(wft.shape),
                  _const_spec(bf.shape), _const_spec(wa.shape), _const_spec(wg.shape)],
        out_specs=[row(FOX_WIDTH), row(FOX_WIDTH), row(FOX_WIDTH),
                   pl.BlockSpec((None, FOX_HEADS, tm), lambda i: (i // tpb, 0, i % tpb)),
                   row(CONV_CHANNELS)],
        out_shape=[jax.ShapeDtypeStruct((t, FOX_WIDTH), BF16)] * 3
        + [jax.ShapeDtypeStruct((batch, FOX_HEADS, seq), F32),
           jax.ShapeDtypeStruct((t, CONV_CHANNELS), F32)],
        scratch_shapes=[pltpu.VMEM((FOX_HEADS, LANES), F32)],
        compiler_params=_cparams(("arbitrary",), 40),
        name="l0_in",
    )(x, g, wqkv, wft, bf, wa, wg)


def _fox_kernel(qi_tab, ki_tab, q_ref, k_ref, v_ref, c_ref, o_ref,
                qa_scr, qb_scr, m_scr, l_scr, acc_scr):
    step = pl.program_id(2)
    qi = qi_tab[step]
    ki = ki_tab[step]
    tq = q_ref.shape[0]
    tk = k_ref.shape[0]
    is_a = lax.broadcasted_iota(I32, (tq, LANES), 1) < FOX_HEAD_DIM

    @pl.when(ki == 0)
    def _():
        q2 = q_ref[...].astype(F32)
        qa_scr[...] = jnp.where(is_a, q2, 0.0).astype(BF16)
        qb_scr[...] = jnp.where(is_a, 0.0, q2).astype(BF16)
        m_scr[...] = jnp.full(m_scr.shape, NEG, F32)
        l_scr[...] = jnp.zeros_like(l_scr)
        acc_scr[...] = jnp.zeros_like(acc_scr)

    k2 = k_ref[...]
    v2 = v_ref[...]
    row = qi * tq + lax.broadcasted_iota(I32, (tq, tk), 0)
    col = ki * tk + lax.broadcasted_iota(I32, (tq, tk), 1)
    causal = col <= row

    def one_head(h, qm):
        s = _dot_nt(qm, k2) - c_ref[h:h + 1, :]
        s = jnp.where(causal, s, NEG)
        m_old = m_scr[h]
        m_new = jnp.maximum(m_old, jnp.max(s, axis=1, keepdims=True))
        p = jnp.exp(s - m_new)
        alpha = jnp.exp(m_old - m_new)
        l_scr[h] = alpha * l_scr[h] + jnp.sum(p, axis=1, keepdims=True)
        m_scr[h] = m_new
        return alpha, _dot(p.astype(BF16), v2)

    alpha_a, pv_a = one_head(0, qa_scr[...])
    alpha_b, pv_b = one_head(1, qb_scr[...])
    acc_scr[...] = acc_scr[...] * jnp.where(is_a, alpha_a, alpha_b) + jnp.where(is_a, pv_a, pv_b)

    @pl.when(ki == qi)
    def _():
        o_ref[...] = (acc_scr[...] / jnp.where(is_a, l_scr[0], l_scr[1])).astype(o_ref.dtype)


def _fox_attention(q, k, v, c, *, batch):
    t = q.shape[0]
    seq = t // batch
    nq = seq // TQ
    pairs = FOX_HEADS // 2
    qi_tab, ki_tab = [], []
    for a in range(nq):
        for b_ in range(a + 1):
            qi_tab.append(a)
            ki_tab.append(b_)
    nsteps = len(qi_tab)
    c4 = c.reshape(batch, pairs, 2, seq)
    grid_spec = pltpu.PrefetchScalarGridSpec(
        num_scalar_prefetch=2,
        grid=(batch, pairs, nsteps),
        in_specs=[
            pl.BlockSpec((TQ, LANES), lambda b, p, s, qt, kt: (b * nq + qt[s], p)),
            pl.BlockSpec((TQ, LANES), lambda b, p, s, qt, kt: (b * nq + kt[s], p)),
            pl.BlockSpec((TQ, LANES), lambda b, p, s, qt, kt: (b * nq + kt[s], p)),
            pl.BlockSpec((None, None, 2, TQ), lambda b, p, s, qt, kt: (b, p, 0, kt[s])),
        ],
        out_specs=pl.BlockSpec((TQ, LANES), lambda b, p, s, qt, kt: (b * nq + qt[s], p)),
        scratch_shapes=[pltpu.VMEM((TQ, LANES), BF16), pltpu.VMEM((TQ, LANES), BF16),
                        pltpu.VMEM((2, TQ, 1), F32), pltpu.VMEM((2, TQ, 1), F32),
                        pltpu.VMEM((TQ, LANES), F32)],
    )
    return pl.pallas_call(
        _fox_kernel,
        grid_spec=grid_spec,
        out_shape=jax.ShapeDtypeStruct((t, FOX_WIDTH), BF16),
        compiler_params=_cparams(("arbitrary", "arbitrary", "arbitrary"), 32),
        name="fox_attention",
    )(jnp.asarray(qi_tab, I32), jnp.asarray(ki_tab, I32), q, k, v, c4)


def _conv_mod_kernel(u_ref, w_ref, b_ref, g_ref, beta_ref, o_ref, win_ref, *, tiles_per_batch):
    i = pl.program_id(0)
    tm = u_ref.shape[0]

    @pl.when(i % tiles_per_batch == 0)
    def _():
        win_ref[0:CONV_HALO, :] = jnp.zeros((CONV_HALO, CONV_CHANNELS), F32)

    win_ref[CONV_HALO:CONV_HALO + tm, :] = u_ref[...]
    first = CONV_HALO - (CONV_WIDTH - 1)
    for c in range(tm // CONV_ROWS):
        r0 = c * CONV_ROWS
        acc = jnp.broadcast_to(b_ref[...], (CONV_ROWS, CONV_CHANNELS))
        for j in range(CONV_WIDTH):
            acc = acc + win_ref[r0 + first + j:r0 + first + j + CONV_ROWS, :] * w_ref[j:j + 1, :]
        mu = jnp.mean(acc, axis=-1, keepdims=True)
        d = acc - mu
        var = jnp.mean(d * d, axis=-1, keepdims=True)
        y = d * lax.rsqrt(var + EPS) * g_ref[...] + beta_ref[...]
        o_ref[r0:r0 + CONV_ROWS, :] = (y * _sigmoid(y)).astype(o_ref.dtype)
    win_ref[0:CONV_HALO, :] = win_ref[tm:tm + CONV_HALO, :]


def _conv_module(u, w, b, g, beta, *, batch):
    t = u.shape[0]
    tm = TM_CONV
    tpb = (t // batch) // tm
    return pl.pallas_call(
        functools.partial(_conv_mod_kernel, tiles_per_batch=tpb),
        grid=(t // tm,),
        in_specs=[pl.BlockSpec((tm, CONV_CHANNELS), lambda i: (i, 0)), _const_spec(w.shape),
                  _const_spec(b.shape), _const_spec(g.shape), _const_spec(beta.shape)],
        out_specs=pl.BlockSpec((tm, CONV_CHANNELS), lambda i: (i, 0)),
        out_shape=jax.ShapeDtypeStruct((t, CONV_CHANNELS), BF16),
        scratch_shapes=[pltpu.VMEM((CONV_HALO + tm, CONV_CHANNELS), F32)],
        compiler_params=_cparams(("arbitrary",), 32),
        name="conv_module",
    )(u, w, b, g, beta)


def _l0_out_ffn_kernel(x_ref, att_ref, u_ref, woa_ref, wou_ref, g_ref, wg_ref, wu_ref, wd_ref, o_ref):
    x1 = x_ref[...] + _dot(att_ref[...], woa_ref[...]) + _dot(u_ref[...], wou_ref[...])
    hb = _rms(x1, g_ref[...]).astype(BF16)
    gate = _dot(hb, wg_ref[...])
    act = (gate * _sigmoid(gate) * _dot(hb, wu_ref[...])).astype(BF16)
    o_ref[...] = x1 + _dot(act, wd_ref[...])


def _l0_out_ffn(x, att, u, woa, wou, g, wg, wu, wd):
    t = x.shape[0]
    tm = TM_FFN0
    row = lambda w: pl.BlockSpec((tm, w), lambda i: (i, 0))
    return pl.pallas_call(
        _l0_out_ffn_kernel,
        grid=(t // tm,),
        in_specs=[row(D_MODEL), row(FOX_WIDTH), row(CONV_CHANNELS), _const_spec(woa.shape),
                  _const_spec(wou.shape), _const_spec(g.shape), _const_spec(wg.shape),
                  _const_spec(wu.shape), _const_spec(wd.shape)],
        out_specs=row(D_MODEL),
        out_shape=jax.ShapeDtypeStruct((t, D_MODEL), F32),
        compiler_params=_cparams(("arbitrary",), 56),
        name="l0_out_ffn",
    )(x, att, u, woa, wou, g, wg, wu, wd)


def _l1_in_kernel(x_ref, g_ref, wuvo_ref, wfc_ref, bfc_ref, wgr_ref, bgr_ref, cw_ref, cb_ref,
                  wq_ref, wkt_ref, q_ref, kt_ref, v_ref, og_ref, lfc_ref, gr_ref, win_ref,
                  *, tiles_per_batch):
    i = pl.program_id(0)
    tm = x_ref.shape[0]
    w = MLSTM_WIDTH
    hb = _rms(x_ref[...], g_ref[...]).astype(BF16)
    uvo = _dot(hb, wuvo_ref[...])
    v_ref[...] = uvo[:, w:2 * w].astype(BF16)
    og_ref[...] = _sigmoid(uvo[:, 2 * w:]).astype(BF16)

    lfc_ref[...] = _log_sigmoid(_dot(hb, wfc_ref[...]) + bfc_ref[...])
    grow = _dot_nt(wgr_ref[...], hb) + bgr_ref[...]
    is_i = lax.broadcasted_iota(I32, grow.shape, 0) < MLSTM_HEADS
    gr_ref[...] = jnp.where(is_i, grow, _log_sigmoid(grow))

    @pl.when(i % tiles_per_batch == 0)
    def _():
        win_ref[0:L1_HALO, :] = jnp.zeros((L1_HALO, w), F32)

    win_ref[L1_HALO:L1_HALO + tm, :] = uvo[:, :w]
    first = L1_HALO - (MLSTM_CONV_WIDTH - 1)
    acc = jnp.broadcast_to(cb_ref[...], (tm, w))
    for j in range(MLSTM_CONV_WIDTH):
        acc = acc + win_ref[first + j:first + j + tm, :] * cw_ref[j:j + 1, :]
    win_ref[0:L1_HALO, :] = win_ref[tm:tm + L1_HALO, :]
    uc = (acc * _sigmoid(acc)).astype(BF16)
    d = MLSTM_HEAD_DIM
    for h in range(MLSTM_HEADS):
        uh = uc[:, h * d:(h + 1) * d]
        q_ref[:, h * d:(h + 1) * d] = _dot(uh, wq_ref[h]).astype(BF16)
        kt_ref[h * d:(h + 1) * d, :] = (_dot_nt(wkt_ref[h], uh) * (d ** -0.5)).astype(BF16)


def _l1_in(x, g, wuvo, wfc, bfc, wgr, bgr, cw, cb, wq, wkt, *, batch):
    t = x.shape[0]
    tm = TM_IN1
    tpb = (t // batch) // tm
    w = MLSTM_WIDTH
    row = lambda n: pl.BlockSpec((tm, n), lambda i: (i, 0))
    col = lambda n: pl.BlockSpec((n, tm), lambda i: (0, i))
    consts = [g, wuvo, wfc, bfc, wgr, bgr, cw, cb, wq, wkt]
    return pl.pallas_call(
        functools.partial(_l1_in_kernel, tiles_per_batch=tpb),
        grid=(t // tm,),
        in_specs=[row(D_MODEL)] + [_const_spec(a.shape) for a in consts],
        out_specs=[row(w), col(w), row(w), row(w), row(LANES), col(2 * MLSTM_HEADS)],
        out_shape=[jax.ShapeDtypeStruct((t, w), BF16), jax.ShapeDtypeStruct((w, t), BF16),
                   jax.ShapeDtypeStruct((t, w), BF16), jax.ShapeDtypeStruct((t, w), BF16),
                   jax.ShapeDtypeStruct((t, LANES), F32),
                   jax.ShapeDtypeStruct((2 * MLSTM_HEADS, t), F32)],
        scratch_shapes=[pltpu.VMEM((L1_HALO + tm, w), F32)],
        compiler_params=_cparams(("arbitrary",), 48),
        name="l1_in",
    )(x, *consts)


def _mlstm_kernel(q_ref, kt_ref, v_ref, og_ref, lfc_ref, gr_ref, hg_ref, o_ref, c_scr, m_scr):
    ci = pl.program_id(1)
    L = q_ref.shape[0]
    d = MLSTM_HEAD_DIM
    nh = MLSTM_HEADS

    @pl.when(ci == 0)
    def _():
        c_scr[...] = jnp.zeros_like(c_scr)
        m_scr[...] = jnp.zeros_like(m_scr)

    r = lax.broadcasted_iota(I32, (L, L), 0)
    c = lax.broadcasted_iota(I32, (L, L), 1)
    tril = r >= c
    tri_lo = jnp.where(tril, 1.0, 0.0).astype(BF16)
    tri_up = jnp.where(r <= c, 1.0, 0.0).astype(BF16)

    bc_col = sum(_dot(tri_lo, p) for p in _split3(lfc_ref[...]))
    bc_row = sum(_dot(p, tri_up) for p in _split3(gr_ref[nh:2 * nh, :]))
    ones_blk = jnp.ones((L, d), BF16)

    for h in range(nh):
        sl = slice(h * d, (h + 1) * d)
        qh = q_ref[:, sl]
        kth = kt_ref[sl, :]
        v_aug = jnp.concatenate([v_ref[:, sl], ones_blk], axis=1)
        bcol = bc_col[:, h:h + 1]
        brow = bc_row[h:h + 1, :]
        srow = gr_ref[h:h + 1, :] - brow
        g = brow[:, L - 1:L]
        m_old = m_scr[h][:, 0:1]
        ct = c_scr[h]

        dmat = jnp.where(tril, bcol + srow, NEG)
        inter = bcol + m_old
        m_t = jnp.maximum(inter, jnp.max(dmat, axis=1, keepdims=True))
        w_inter = jnp.exp(inter - m_t)
        w_intra = jnp.exp(dmat - m_t) * _dot(qh, kth)
        res = w_inter * _dot(qh, ct.astype(BF16)) + _dot(w_intra.astype(BF16), v_aug)
        den = jnp.maximum(jnp.abs(res[:, d:]), jnp.exp(-m_t))
        hc = og_ref[:, sl].astype(F32) * (res[:, :d] / den)
        o_ref[:, sl] = _rms(hc, hg_ref[:, sl]).astype(o_ref.dtype)

        a_row = g + srow
        m_new = jnp.maximum(g + m_old, jnp.max(a_row, axis=1, keepdims=True))
        w_in = jnp.exp(a_row - m_new)
        kw = (kth.astype(F32) * w_in).astype(BF16)
        c_scr[h] = jnp.exp(g + m_old - m_new) * ct + _dot(kw, v_aug)
        m_scr[h] = jnp.broadcast_to(m_new, (1, LANES))


def _mlstm(q, kt, v, og, lfc, gr, hg, *, batch):
    t = q.shape[0]
    w = MLSTM_WIDTH
    nc = (t // batch) // CHUNK
    row = lambda n: pl.BlockSpec((CHUNK, n), lambda b, c: (b * nc + c, 0))
    col = lambda n: pl.BlockSpec((n, CHUNK), lambda b, c: (0, b * nc + c))
    return pl.pallas_call(
        _mlstm_kernel,
        grid=(batch, nc),
        in_specs=[row(w), col(w), row(w), row(w), row(LANES), col(2 * MLSTM_HEADS), _const_spec(hg.shape)],
        out_specs=row(w),
        out_shape=jax.ShapeDtypeStruct((t, w), BF16),
        scratch_shapes=[pltpu.VMEM((MLSTM_HEADS, MLSTM_HEAD_DIM, 2 * MLSTM_HEAD_DIM), F32),
                        pltpu.VMEM((MLSTM_HEADS, 1, LANES), F32)],
        compiler_params=_cparams(("arbitrary", "arbitrary"), 32),
        name="mlstm",
    )(q, kt, v, og, lfc, gr, hg)


def _l1_out_router_kernel(x_ref, hc_ref, wo_ref, g_ref, wr_ref, x3_ref, h3_ref, meta_ref, cnt_ref, carry_ref):
    i = pl.program_id(0)
    tm = x_ref.shape[0]

    @pl.when(i == 0)
    def _():
        carry_ref[...] = jnp.zeros_like(carry_ref)

    x3 = x_ref[...] + _dot(hc_ref[...], wo_ref[...])
    x3_ref[...] = x3
    h3 = _rms(x3, g_ref[...])
    h3_ref[...] = h3

    h_hi, h_mid, _ = _split3(h3)
    w_hi, w_mid, _ = _split3(wr_ref[...])
    logits = _dot(h_hi, w_hi) + (_dot(h_hi, w_mid) + _dot(h_mid, w_hi))

    lane = lax.broadcasted_iota(I32, (tm, LANES), 1)
    lane_f = lane.astype(F32)
    lg = jnp.where(lane < N_EXPERTS, logits, NEG)
    v1 = jnp.max(lg, axis=1, keepdims=True)
    i1 = jnp.min(jnp.where(lg == v1, lane_f, float(LANES)), axis=1, keepdims=True)
    lg2 = jnp.where(lane_f == i1, NEG, lg)
    v2 = jnp.max(lg2, axis=1, keepdims=True)
    i2 = jnp.min(jnp.where(lg2 == v2, lane_f, float(LANES)), axis=1, keepdims=True)
    e = jnp.exp(v2 - v1)
    g1 = 1.0 / (1.0 + e)
    g2 = e / (1.0 + e)

    oh1 = lane_f == i1
    oh2 = lane_f == i2
    oh = jnp.where(oh1 | oh2, 1.0, 0.0)
    r = lax.broadcasted_iota(I32, (tm, tm), 0)
    c = lax.broadcasted_iota(I32, (tm, tm), 1)
    strict = jnp.where(c < r, 1.0, 0.0).astype(BF16)
    pos = _dot(strict, oh.astype(BF16)) + carry_ref[0:1, :]
    rank1 = jnp.sum(jnp.where(oh1, pos, 0.0), axis=1, keepdims=True)
    rank2 = jnp.sum(jnp.where(oh2, pos, 0.0), axis=1, keepdims=True)
    total = carry_ref[0:1, :] + jnp.sum(oh, axis=0, keepdims=True)
    carry_ref[...] = jnp.broadcast_to(total, carry_ref.shape)
    cnt_ref[...] = jnp.broadcast_to(total, cnt_ref.shape)

    meta = jnp.zeros((tm, LANES), F32)
    for k, val in enumerate((i1, i2, g1, g2, rank1, rank2)):
        meta = jnp.where(lane == k, val, meta)
    meta_ref[...] = meta


def _l1_out_router(x, hc, wo, g, wr):
    t = x.shape[0]
    tm = TM_OUT1
    row = lambda n: pl.BlockSpec((tm, n), lambda i: (i, 0))
    return pl.pallas_call(
        _l1_out_router_kernel,
        grid=(t // tm,),
        in_specs=[row(D_MODEL), row(MLSTM_WIDTH), _const_spec(wo.shape), _const_spec(g.shape),
                  _const_spec(wr.shape)],
        out_specs=[row(D_MODEL), row(D_MODEL), row(LANES), pl.BlockSpec((SUBLANES, LANES), lambda i: (0, 0))],
        out_shape=[jax.ShapeDtypeStruct((t, D_MODEL), F32), jax.ShapeDtypeStruct((t, D_MODEL), F32),
                   jax.ShapeDtypeStruct((t, LANES), F32), jax.ShapeDtypeStruct((SUBLANES, LANES), F32)],
        scratch_shapes=[pltpu.VMEM((SUBLANES, LANES), F32)],
        compiler_params=_cparams(("arbitrary",), 32),
        name="l1_out_router",
    )(x, hc, wo, g, wr)


def _row_copy(src_hbm, dst_hbm, r_src, r_dst, sem):
    return pltpu.make_async_copy(src_hbm.at[pl.ds(r_src, 1), :], dst_hbm.at[pl.ds(r_dst, 1), :], sem)


def _experts_kernel(te_ref, nused_ref, nrows_ref, src_ref, dst_ref, h_hbm, wg_ref, wu_ref, wd_ref, y_hbm,
                    x_scr, xb_scr, acc_scr, o_scr, gsem, ssem):
    i = pl.program_id(0)
    j = pl.program_id(1)
    nf = pl.num_programs(1)
    tm = x_scr.shape[0]
    valid = i < nused_ref[0]
    nrows = nrows_ref[i]

    @pl.when(valid & (j == 0))
    def _():
        def start(r, _):
            _row_copy(h_hbm, x_scr, src_ref[0, r], r, gsem).start()
            return _
        lax.fori_loop(0, tm, start, None)

        def wait(r, _):
            _row_copy(h_hbm, x_scr, 0, r, gsem).wait()
            return _
        lax.fori_loop(0, tm, wait, None)
        xb_scr[...] = x_scr[...].astype(BF16)
        acc_scr[...] = jnp.zeros_like(acc_scr)

    @pl.when(valid)
    def _():
        xb = xb_scr[...]
        gate = _dot(xb, wg_ref[...])
        act = (gate * _sigmoid(gate) * _dot(xb, wu_ref[...])).astype(BF16)
        acc_scr[...] += _dot(act, wd_ref[...])

    @pl.when(valid & (j == nf - 1))
    def _():
        o_scr[...] = acc_scr[...]

        def start(r, _):
            _row_copy(o_scr, y_hbm, r, dst_ref[0, r], ssem).start()
            return _
        lax.fori_loop(0, nrows, start, None)

        def wait(r, _):
            _row_copy(o_scr, y_hbm, r, 0, ssem).wait()
            return _
        lax.fori_loop(0, nrows, wait, None)


def _experts(tile_expert, nused, nrows, src, dst, h3, wg, wu, wd, n_out_rows):
    nt = src.shape[0]
    nf = D_FF_EXPERT // TF_E

    def wcol(i, j, te, nu, nr):
        return (te[i], 0, jnp.where(i < nu[0], j, nf - 1))

    def wrow(i, j, te, nu, nr):
        return (te[i], jnp.where(i < nu[0], j, nf - 1), 0)

    smem_tile = pl.BlockSpec((None, 1, TM_E), lambda i, j, te, nu, nr: (i, 0, 0), memory_space=pltpu.SMEM)
    grid_spec = pltpu.PrefetchScalarGridSpec(
        num_scalar_prefetch=3,
        grid=(nt, nf),
        in_specs=[smem_tile, smem_tile, pl.BlockSpec(memory_space=pl.ANY),
                  pl.BlockSpec((None, D_MODEL, TF_E), wcol), pl.BlockSpec((None, D_MODEL, TF_E), wcol),
                  pl.BlockSpec((None, TF_E, D_MODEL), wrow)],
        out_specs=pl.BlockSpec(memory_space=pl.ANY),
        scratch_shapes=[pltpu.VMEM((TM_E, D_MODEL), F32), pltpu.VMEM((TM_E, D_MODEL), BF16),
                        pltpu.VMEM((TM_E, D_MODEL), F32), pltpu.VMEM((TM_E, D_MODEL), F32),
                        pltpu.SemaphoreType.DMA(()), pltpu.SemaphoreType.DMA(())],
    )
    return pl.pallas_call(
        _experts_kernel,
        grid_spec=grid_spec,
        out_shape=jax.ShapeDtypeStruct((n_out_rows, D_MODEL), F32),
        compiler_params=_cparams(("arbitrary", "arbitrary"), 48),
        name="experts",
    )(tile_expert, nused, nrows, src, dst, h3, wg, wu, wd)


def _combine_kernel(x_ref, y0_ref, y1_ref, meta_ref, g_ref, o_ref):
    meta = meta_ref[...]
    y = x_ref[...] + meta[:, 2:3] * y0_ref[...] + meta[:, 3:4] * y1_ref[...]
    o_ref[...] = _rms(y, g_ref[...])


def _combine(x3, y, meta, g):
    t = x3.shape[0]
    tm = TM_FIN
    nt = t // tm
    row = lambda n: pl.BlockSpec((tm, n), lambda i: (i, 0))
    return pl.pallas_call(
        _combine_kernel,
        grid=(nt,),
        in_specs=[row(D_MODEL), row(D_MODEL), pl.BlockSpec((tm, D_MODEL), lambda i: (nt + i, 0)),
                  row(LANES), _const_spec(g.shape)],
        out_specs=row(D_MODEL),
        out_shape=jax.ShapeDtypeStruct((t, D_MODEL), F32),
        compiler_params=_cparams(("arbitrary",), 32),
        name="combine_final_norm",
    )(x3, y, y, meta, g)


def _route(meta, cnt, t):
    idx = meta[:, 0:2].astype(I32)
    rank = meta[:, 4:6].astype(I32)
    counts = cnt[0, :N_EXPERTS].astype(I32)
    tiles_e = (counts + TM_E - 1) // TM_E
    tile_end = jnp.cumsum(tiles_e)
    row_off = (tile_end - tiles_e) * TM_E
    nt = 2 * t // TM_E + N_EXPERTS
    p = nt * TM_E
    pos = (row_off[idx] + rank).reshape(-1)
    tok = jnp.repeat(jnp.arange(t, dtype=I32), 2)
    slot = jnp.tile(jnp.arange(2, dtype=I32), t)
    src = jnp.zeros((p,), I32).at[pos].set(tok)
    dst = jnp.zeros((p,), I32).at[pos].set(slot * t + tok)
    tiles = jnp.arange(nt, dtype=I32)
    tile_expert = jnp.minimum(jnp.searchsorted(tile_end, tiles, side="right"), N_EXPERTS - 1).astype(I32)
    first_tile = (tile_end - tiles_e)[tile_expert]
    nrows = jnp.clip(counts[tile_expert] - (tiles - first_tile) * TM_E, 0, TM_E)
    nused = tile_end[-1:].astype(I32)
    nrows = jnp.where(tiles < nused[0], nrows, 0).astype(I32)
    return tile_expert, nused, nrows, src.reshape(nt, 1, TM_E), dst.reshape(nt, 1, TM_E), 2 * t


def kernel(x, l0_norm1_g, l0_w_in, l0_b_f, l0_dw_w, l0_dw_b, l0_cln_g, l0_cln_b, l0_w_out, l0_norm2_g, l0_w_gate, l0_w_up, l0_w_down, l1_norm1_g, l1_w_in, l1_b_i, l1_b_f, l1_conv_w, l1_conv_b, l1_wq_head, l1_wk_head, l1_hnorm_g, l1_w_out, l1_norm2_g, l1_router, l1_e_gate, l1_e_up, l1_e_down, final_norm_g):
    batch, seq, d = x.shape
    t = batch * seq
    xf = x.reshape(t, d)
    vec = lambda a: a.reshape(1, -1).astype(F32)
    bf = lambda a: a.astype(BF16)

    fw = FOX_WIDTH
    o_f = 3 * fw
    o_a = o_f + FOX_HEADS
    o_g = o_a + CONV_CHANNELS
    q, k, v, c, u = _l0_in(
        xf, vec(l0_norm1_g), bf(l0_w_in[:, :o_f]), bf(l0_w_in[:, o_f:o_a].T),
        l0_b_f.reshape(-1, 1).astype(F32), bf(l0_w_in[:, o_a:o_g]), bf(l0_w_in[:, o_g:]), batch=batch)
    att = _fox_attention(q, k, v, c, batch=batch)
    dw_w = jnp.concatenate([l0_dw_w, jnp.zeros((1, CONV_CHANNELS), F32)], axis=0)
    uc = _conv_module(u, dw_w, vec(l0_dw_b), vec(l0_cln_g), vec(l0_cln_b), batch=batch)
    x2 = _l0_out_ffn(xf, att, uc, bf(l0_w_out[:fw]), bf(l0_w_out[fw:]), vec(l0_norm2_g),
                     bf(l0_w_gate), bf(l0_w_up), bf(l0_w_down))

    w = MLSTM_WIDTH
    nh = MLSTM_HEADS
    w_i = l1_w_in[:, 3 * w:3 * w + nh]
    w_f = l1_w_in[:, 3 * w + nh:]
    wfc = bf(jnp.pad(w_f, ((0, 0), (0, LANES - nh))))
    bfc = jnp.pad(l1_b_f, (0, LANES - nh)).reshape(1, LANES).astype(F32)
    wgr = bf(jnp.concatenate([w_i.T, w_f.T], axis=0))
    bgr = jnp.concatenate([l1_b_i, l1_b_f]).reshape(-1, 1).astype(F32)
    ql, kt, vl, og, lfc, gr = _l1_in(
        x2, vec(l1_norm1_g), bf(l1_w_in[:, :3 * w]), wfc, bfc, wgr, bgr, l1_conv_w.astype(F32),
        vec(l1_conv_b), bf(l1_wq_head), bf(jnp.swapaxes(l1_wk_head, 1, 2)), batch=batch)
    hc = _mlstm(ql, kt, vl, og, lfc, gr, vec(l1_hnorm_g), batch=batch)
    wr = jnp.pad(l1_router.astype(F32), ((0, 0), (0, LANES - N_EXPERTS)))
    x3, h3, meta, cnt = _l1_out_router(x2, hc, bf(l1_w_out), vec(l1_norm2_g), wr)

    tile_expert, nused, nrows, src, dst, n_rows = _route(meta, cnt, t)
    y = _experts(tile_expert, nused, nrows, src, dst, h3, bf(l1_e_gate), bf(l1_e_up), bf(l1_e_down), n_rows)
    out = _combine(x3, y, meta, vec(final_norm_g))
    return out.reshape(batch, seq, d)
```

```python
import functools

import jax
import jax.numpy as jnp
from jax import lax
from jax.experimental import pallas as pl
from jax.experimental.pallas import tpu as pltpu

F32 = jnp.float32
BF16 = jnp.bfloat16
I32 = jnp.int32

EPS = 1e-6
NEG = -1e30
LOG2E = 1.4426950408889634

D_MODEL = 1024
FOX_HEADS = 8
FOX_HEAD_DIM = 64
FOX_WIDTH = FOX_HEADS * FOX_HEAD_DIM
CONV_CHANNELS = 512
CONV_WIDTH = 31
MLSTM_HEADS = 8
MLSTM_HEAD_DIM = 128
MLSTM_WIDTH = MLSTM_HEADS * MLSTM_HEAD_DIM
MLSTM_CONV_WIDTH = 4
D_FF = 2816
N_EXPERTS = 8
D_FF_EXPERT = 3584

LANES = 128
SUBLANES = 8
MIB = 1024 * 1024

TM_IN0 = 512
TQ = 512
FOX_KAUG = 256
TM_CONV = 512
CONV_ROWS = 64
CONV_HALO = 32
TM_FFN0 = 512
TM_IN1 = 512
L1_HALO = 8
CHUNK = 256
TM_OUT1 = 512
TM_E = 512
TF_E = 512
TM_FIN = 512


def _cparams(sem, vmem_mib):
    return pltpu.CompilerParams(dimension_semantics=sem, vmem_limit_bytes=vmem_mib * MIB)


def _rms(x, g):
    return x * lax.rsqrt(jnp.mean(x * x, axis=-1, keepdims=True) + EPS) * g


def _sigmoid(x):
    return 1.0 / (1.0 + jnp.exp(-x))


def _log_sigmoid(x):
    return jnp.minimum(x, 0.0) - jnp.log(1.0 + jnp.exp(-jnp.abs(x)))


def _dot(a, b):
    return jnp.dot(a, b, preferred_element_type=F32)


def _dot_nt(a, b):
    return lax.dot_general(a, b, (((1,), (1,)), ((), ())), preferred_element_type=F32)


def _split3(x):
    hi = x.astype(BF16)
    r1 = x - hi.astype(F32)
    mid = r1.astype(BF16)
    lo = (r1 - mid.astype(F32)).astype(BF16)
    return hi, mid, lo


def _const_spec(shape):
    nd = len(shape)
    return pl.BlockSpec(shape, lambda *_: (0,) * nd, pipeline_mode=pl.Buffered(1))


def _l0_in_kernel(x_ref, g_ref, wqv_ref, wkt_ref, wft_ref, bf_ref, wa_ref, wg_ref,
                  q_ref, kta_ref, v_ref, u_ref, carry_ref, *, tiles_per_batch):
    i = pl.program_id(0)
    tm = x_ref.shape[0]
    hb = _rms(x_ref[...], g_ref[...]).astype(BF16)
    qv = _dot(hb, wqv_ref[...])
    q_ref[...] = (qv[:, :FOX_WIDTH] * (FOX_HEAD_DIM ** -0.5 * LOG2E)).astype(BF16)
    v_ref[...] = qv[:, FOX_WIDTH:].astype(BF16)
    kt = _dot_nt(wkt_ref[...], hb).astype(BF16)
    u_ref[...] = _dot(hb, wa_ref[...]) * _sigmoid(_dot(hb, wg_ref[...]))

    logf = _log_sigmoid(_dot_nt(wft_ref[...], hb) + bf_ref[...])
    lane = lax.broadcasted_iota(I32, logf.shape, 1)
    cum = logf
    shift = 1
    while shift < tm:
        cum = cum + jnp.where(lane >= shift, pltpu.roll(cum, shift, 1), 0.0)
        shift *= 2

    @pl.when(i % tiles_per_batch == 0)
    def _():
        carry_ref[...] = jnp.zeros_like(carry_ref)

    c = cum + carry_ref[:, 0:1]
    carry_ref[...] = jnp.broadcast_to(c[:, tm - 1:tm], carry_ref.shape)

    pieces = [p.astype(F32) for p in _split3(c * LOG2E)]
    sub = lax.broadcasted_iota(I32, (2 * SUBLANES, tm), 0)
    zeros_tail = jnp.zeros((FOX_KAUG - LANES - 2 * SUBLANES, tm), BF16)
    for p in range(FOX_HEADS // 2):
        ext = jnp.zeros((2 * SUBLANES, tm), F32)
        for hh in range(2):
            for n, piece in enumerate(pieces):
                ext = jnp.where(sub == 3 * hh + n, piece[2 * p + hh:2 * p + hh + 1, :], ext)
        kta_ref[p, 0:LANES, :] = kt[p * LANES:(p + 1) * LANES, :]
        kta_ref[p, LANES:LANES + 2 * SUBLANES, :] = ext.astype(BF16)
        kta_ref[p, LANES + 2 * SUBLANES:, :] = zeros_tail


def _l0_in(x, g, wqv, wkt, wft, bf, wa, wg, *, batch):
    t = x.shape[0]
    tm = TM_IN0
    tpb = (t // batch) // tm
    pairs = FOX_HEADS // 2
    row = lambda w: pl.BlockSpec((tm, w), lambda i: (i, 0))
    consts = [g, wqv, wkt, wft, bf, wa, wg]
    return pl.pallas_call(
        functools.partial(_l0_in_kernel, tiles_per_batch=tpb),
        grid=(t // tm,),
        in_specs=[row(D_MODEL)] + [_const_spec(a.shape) for a in consts],
        out_specs=[row(FOX_WIDTH), pl.BlockSpec((pairs, FOX_KAUG, tm), lambda i: (0, 0, i)),
                   row(FOX_WIDTH), row(CONV_CHANNELS)],
        out_shape=[jax.ShapeDtypeStruct((t, FOX_WIDTH), BF16),
                   jax.ShapeDtypeStruct((pairs, FOX_KAUG, t), BF16),
                   jax.ShapeDtypeStruct((t, FOX_WIDTH), BF16),
                   jax.ShapeDtypeStruct((t, CONV_CHANNELS), F32)],
        scratch_shapes=[pltpu.VMEM((FOX_HEADS, LANES), F32)],
        compiler_params=_cparams(("arbitrary",), 40),
        name="l0_in",
    )(x, *consts)


def _fox_kernel(q_ref, kta_ref, v_ref, o_ref, qa_scr, qb_scr, m_scr, acc_scr):
    qi = pl.program_id(2)
    tq = q_ref.shape[0]
    tk = tq
    lane = lax.broadcasted_iota(I32, (tq, LANES), 1)
    is_a = lane < FOX_HEAD_DIM
    lane_k = lax.broadcasted_iota(I32, (tk, LANES), 1)

    q2 = q_ref[...].astype(F32)
    ext_a = jnp.where(lane < 3, -1.0, 0.0)
    ext_b = jnp.where((lane >= 3) & (lane < 6), -1.0, 0.0)
    qa_scr[...] = jnp.concatenate([jnp.where(is_a, q2, 0.0), ext_a], axis=1).astype(BF16)
    qb_scr[...] = jnp.concatenate([jnp.where(is_a, 0.0, q2), ext_b], axis=1).astype(BF16)
    m_scr[...] = jnp.full(m_scr.shape, NEG, F32)
    acc_scr[...] = jnp.zeros_like(acc_scr)

    def tile(ki, masked):
        start = pl.multiple_of(ki * tk, tk)
        kt = kta_ref[:, pl.ds(start, tk)]
        vf = v_ref[pl.ds(start, tk), :].astype(F32)
        v_augs = (jnp.where(lane_k < FOX_HEAD_DIM, vf, 1.0).astype(BF16),
                  jnp.where(lane_k < FOX_HEAD_DIM, 1.0, vf).astype(BF16))
        if masked:
            causal = (lax.broadcasted_iota(I32, (tq, tk), 1) <= lax.broadcasted_iota(I32, (tq, tk), 0))
        for h, q_scr in enumerate((qa_scr, qb_scr)):
            s = _dot(q_scr[...], kt)
            if masked:
                s = jnp.where(causal, s, NEG)
            m_old = m_scr[h]
            m_new = jnp.maximum(m_old, jnp.max(s, axis=1, keepdims=True))
            p = jnp.concatenate(
                [jnp.exp2(s[:, j * LANES:(j + 1) * LANES] - m_new).astype(BF16) for j in range(tk // LANES)],
                axis=1)
            acc_scr[h] = acc_scr[h] * jnp.exp2(m_old - m_new) + _dot(p, v_augs[h])
            m_scr[h] = m_new

    def body(ki, carry):
        tile(ki, False)
        return carry

    lax.fori_loop(0, qi, body, 0)
    tile(qi, True)

    acc_a = acc_scr[0]
    acc_b = acc_scr[1]
    o = jnp.where(is_a, acc_a / pltpu.roll(acc_a, FOX_HEAD_DIM, 1), acc_b / pltpu.roll(acc_b, FOX_HEAD_DIM, 1))
    o_ref[...] = o.astype(o_ref.dtype)


def _fox_attention(q, kta, v, *, batch):
    t = q.shape[0]
    seq = t // batch
    nq = seq // TQ
    pairs = FOX_HEADS // 2
    return pl.pallas_call(
        _fox_kernel,
        grid=(batch, pairs, nq),
        in_specs=[pl.BlockSpec((TQ, LANES), lambda b, p, i: (b * nq + i, p)),
                  pl.BlockSpec((None, FOX_KAUG, seq), lambda b, p, i: (p, 0, b)),
                  pl.BlockSpec((seq, LANES), lambda b, p, i: (b, p))],
        out_specs=pl.BlockSpec((TQ, LANES), lambda b, p, i: (b * nq + i, p)),
        out_shape=jax.ShapeDtypeStruct((t, FOX_WIDTH), BF16),
        scratch_shapes=[pltpu.VMEM((TQ, FOX_KAUG), BF16), pltpu.VMEM((TQ, FOX_KAUG), BF16),
                        pltpu.VMEM((2, TQ, LANES), F32), pltpu.VMEM((2, TQ, LANES), F32)],
        compiler_params=_cparams(("arbitrary", "arbitrary", "arbitrary"), 40),
        name="fox_attention",
    )(q, kta, v)


def _conv_mod_kernel(u_ref, w_ref, b_ref, g_ref, beta_ref, o_ref, win_ref, *, tiles_per_batch):
    i = pl.program_id(0)
    tm = u_ref.shape[0]

    @pl.when(i % tiles_per_batch == 0)
    def _():
        win_ref[0:CONV_HALO, :] = jnp.zeros((CONV_HALO, CONV_CHANNELS), F32)

    win_ref[CONV_HALO:CONV_HALO + tm, :] = u_ref[...]
    first = CONV_HALO - (CONV_WIDTH - 1)
    for c in range(tm // CONV_ROWS):
        r0 = c * CONV_ROWS
        acc = jnp.broadcast_to(b_ref[...], (CONV_ROWS, CONV_CHANNELS))
        for j in range(CONV_WIDTH):
            acc = acc + win_ref[r0 + first + j:r0 + first + j + CONV_ROWS, :] * w_ref[j:j + 1, :]
        mu = jnp.mean(acc, axis=-1, keepdims=True)
        d = acc - mu
        var = jnp.mean(d * d, axis=-1, keepdims=True)
        y = d * lax.rsqrt(var + EPS) * g_ref[...] + beta_ref[...]
        o_ref[r0:r0 + CONV_ROWS, :] = (y * _sigmoid(y)).astype(o_ref.dtype)
    win_ref[0:CONV_HALO, :] = win_ref[tm:tm + CONV_HALO, :]


def _conv_module(u, w, b, g, beta, *, batch):
    t = u.shape[0]
    tm = TM_CONV
    tpb = (t // batch) // tm
    return pl.pallas_call(
        functools.partial(_conv_mod_kernel, tiles_per_batch=tpb),
        grid=(t // tm,),
        in_specs=[pl.BlockSpec((tm, CONV_CHANNELS), lambda i: (i, 0)), _const_spec(w.shape),
                  _const_spec(b.shape), _const_spec(g.shape), _const_spec(beta.shape)],
        out_specs=pl.BlockSpec((tm, CONV_CHANNELS), lambda i: (i, 0)),
        out_shape=jax.ShapeDtypeStruct((t, CONV_CHANNELS), BF16),
        scratch_shapes=[pltpu.VMEM((CONV_HALO + tm, CONV_CHANNELS), F32)],
        compiler_params=_cparams(("arbitrary",), 32),
        name="conv_module",
    )(u, w, b, g, beta)


def _l0_out_ffn_kernel(x_ref, att_ref, u_ref, woa_ref, wou_ref, g_ref, wg_ref, wu_ref, wd_ref, o_ref):
    x1 = x_ref[...] + _dot(att_ref[...], woa_ref[...]) + _dot(u_ref[...], wou_ref[...])
    hb = _rms(x1, g_ref[...]).astype(BF16)
    gate = _dot(hb, wg_ref[...])
    act = (gate * _sigmoid(gate) * _dot(hb, wu_ref[...])).astype(BF16)
    o_ref[...] = x1 + _dot(act, wd_ref[...])


def _l0_out_ffn(x, att, u, woa, wou, g, wg, wu, wd):
    t = x.shape[0]
    tm = TM_FFN0
    row = lambda w: pl.BlockSpec((tm, w), lambda i: (i, 0))
    return pl.pallas_call(
        _l0_out_ffn_kernel,
        grid=(t // tm,),
        in_specs=[row(D_MODEL), row(FOX_WIDTH), row(CONV_CHANNELS), _const_spec(woa.shape),
                  _const_spec(wou.shape), _const_spec(g.shape), _const_spec(wg.shape),
                  _const_spec(wu.shape), _const_spec(wd.shape)],
        out_specs=row(D_MODEL),
        out_shape=jax.ShapeDtypeStruct((t, D_MODEL), F32),
        compiler_params=_cparams(("arbitrary",), 56),
        name="l0_out_ffn",
    )(x, att, u, woa, wou, g, wg, wu, wd)


def _l1_in_kernel(x_ref, g_ref, wuvo_ref, wfc_ref, bfc_ref, wgr_ref, bgr_ref, cw_ref, cb_ref,
                  wq_ref, wkt_ref, q_ref, kt_ref, v_ref, og_ref, lfc_ref, gr_ref, win_ref,
                  *, tiles_per_batch):
    i = pl.program_id(0)
    tm = x_ref.shape[0]
    w = MLSTM_WIDTH
    hb = _rms(x_ref[...], g_ref[...]).astype(BF16)
    uvo = _dot(hb, wuvo_ref[...])
    v_ref[...] = uvo[:, w:2 * w].astype(BF16)
    og_ref[...] = _sigmoid(uvo[:, 2 * w:]).astype(BF16)

    lfc_ref[...] = _log_sigmoid(_dot(hb, wfc_ref[...]) + bfc_ref[...])
    grow = _dot_nt(wgr_ref[...], hb) + bgr_ref[...]
    is_i = lax.broadcasted_iota(I32, grow.shape, 0) < MLSTM_HEADS
    gr_ref[...] = jnp.where(is_i, grow, _log_sigmoid(grow))

    @pl.when(i % tiles_per_batch == 0)
    def _():
        win_ref[0:L1_HALO, :] = jnp.zeros((L1_HALO, w), F32)

    win_ref[L1_HALO:L1_HALO + tm, :] = uvo[:, :w]
    first = L1_HALO - (MLSTM_CONV_WIDTH - 1)
    acc = jnp.broadcast_to(cb_ref[...], (tm, w))
    for j in range(MLSTM_CONV_WIDTH):
        acc = acc + win_ref[first + j:first + j + tm, :] * cw_ref[j:j + 1, :]
    win_ref[0:L1_HALO, :] = win_ref[tm:tm + L1_HALO, :]
    uc = (acc * _sigmoid(acc)).astype(BF16)
    d = MLSTM_HEAD_DIM
    for h in range(MLSTM_HEADS):
        uh = uc[:, h * d:(h + 1) * d]
        q_ref[:, h * d:(h + 1) * d] = _dot(uh, wq_ref[h]).astype(BF16)
        kt_ref[h * d:(h + 1) * d, :] = (_dot_nt(wkt_ref[h], uh) * (d ** -0.5)).astype(BF16)


def _l1_in(x, g, wuvo, wfc, bfc, wgr, bgr, cw, cb, wq, wkt, *, batch):
    t = x.shape[0]
    tm = TM_IN1
    tpb = (t // batch) // tm
    w = MLSTM_WIDTH
    row = lambda n: pl.BlockSpec((tm, n), lambda i: (i, 0))
    col = lambda n: pl.BlockSpec((n, tm), lambda i: (0, i))
    consts = [g, wuvo, wfc, bfc, wgr, bgr, cw, cb, wq, wkt]
    return pl.pallas_call(
        functools.partial(_l1_in_kernel, tiles_per_batch=tpb),
        grid=(t // tm,),
        in_specs=[row(D_MODEL)] + [_const_spec(a.shape) for a in consts],
        out_specs=[row(w), col(w), row(w), row(w), row(LANES), col(2 * MLSTM_HEADS)],
        out_shape=[jax.ShapeDtypeStruct((t, w), BF16), jax.ShapeDtypeStruct((w, t), BF16),
                   jax.ShapeDtypeStruct((t, w), BF16), jax.ShapeDtypeStruct((t, w), BF16),
                   jax.ShapeDtypeStruct((t, LANES), F32),
                   jax.ShapeDtypeStruct((2 * MLSTM_HEADS, t), F32)],
        scratch_shapes=[pltpu.VMEM((L1_HALO + tm, w), F32)],
        compiler_params=_cparams(("arbitrary",), 48),
        name="l1_in",
    )(x, *consts)


def _mlstm_kernel(q_ref, kt_ref, v_ref, og_ref, lfc_ref, gr_ref, hg_ref, o_ref, c_scr, m_scr):
    ci = pl.program_id(1)
    L = q_ref.shape[0]
    d = MLSTM_HEAD_DIM
    nh = MLSTM_HEADS

    @pl.when(ci == 0)
    def _():
        c_scr[...] = jnp.zeros_like(c_scr)
        m_scr[...] = jnp.zeros_like(m_scr)

    r = lax.broadcasted_iota(I32, (L, L), 0)
    c = lax.broadcasted_iota(I32, (L, L), 1)
    tril = r >= c
    tri_lo = jnp.where(tril, 1.0, 0.0).astype(BF16)
    tri_up = jnp.where(r <= c, 1.0, 0.0).astype(BF16)

    bc_col = sum(_dot(tri_lo, p) for p in _split3(lfc_ref[...]))
    bc_row = sum(_dot(p, tri_up) for p in _split3(gr_ref[nh:2 * nh, :]))
    ones_blk = jnp.ones((L, d), BF16)

    for h in range(nh):
        sl = slice(h * d, (h + 1) * d)
        qh = q_ref[:, sl]
        kth = kt_ref[sl, :]
        v_aug = jnp.concatenate([v_ref[:, sl], ones_blk], axis=1)
        bcol = bc_col[:, h:h + 1]
        brow = bc_row[h:h + 1, :]
        srow = gr_ref[h:h + 1, :] - brow
        g = brow[:, L - 1:L]
        m_old = m_scr[h][:, 0:1]
        ct = c_scr[h]

        dmat = jnp.where(tril, bcol + srow, NEG)
        inter = bcol + m_old
        m_t = jnp.maximum(inter, jnp.max(dmat, axis=1, keepdims=True))
        w_inter = jnp.exp(inter - m_t)
        w_intra = jnp.exp(dmat - m_t) * _dot(qh, kth)
        res = w_inter * _dot(qh, ct.astype(BF16)) + _dot(w_intra.astype(BF16), v_aug)
        den = jnp.maximum(jnp.abs(res[:, d:]), jnp.exp(-m_t))
        hc = og_ref[:, sl].astype(F32) * (res[:, :d] / den)
        o_ref[:, sl] = _rms(hc, hg_ref[:, sl]).astype(o_ref.dtype)

        a_row = g + srow
        m_new = jnp.maximum(g + m_old, jnp.max(a_row, axis=1, keepdims=True))
        w_in = jnp.exp(a_row - m_new)
        kw = (kth.astype(F32) * w_in).astype(BF16)
        c_scr[h] = jnp.exp(g + m_old - m_new) * ct + _dot(kw, v_aug)
        m_scr[h] = jnp.broadcast_to(m_new, (1, LANES))


def _mlstm(q, kt, v, og, lfc, gr, hg, *, batch):
    t = q.shape[0]
    w = MLSTM_WIDTH
    nc = (t // batch) // CHUNK
    row = lambda n: pl.BlockSpec((CHUNK, n), lambda b, c: (b * nc + c, 0))
    col = lambda n: pl.BlockSpec((n, CHUNK), lambda b, c: (0, b * nc + c))
    return pl.pallas_call(
        _mlstm_kernel,
        grid=(batch, nc),
        in_specs=[row(w), col(w), row(w), row(w), row(LANES), col(2 * MLSTM_HEADS), _const_spec(hg.shape)],
        out_specs=row(w),
        out_shape=jax.ShapeDtypeStruct((t, w), BF16),
        scratch_shapes=[pltpu.VMEM((MLSTM_HEADS, MLSTM_HEAD_DIM, 2 * MLSTM_HEAD_DIM), F32),
                        pltpu.VMEM((MLSTM_HEADS, 1, LANES), F32)],
        compiler_params=_cparams(("arbitrary", "arbitrary"), 32),
        name="mlstm",
    )(q, kt, v, og, lfc, gr, hg)


def _l1_out_router_kernel(x_ref, hc_ref, wo_ref, g_ref, wr_ref, x3_ref, h3_ref, meta_ref, cnt_ref, carry_ref):
    i = pl.program_id(0)
    tm = x_ref.shape[0]

    @pl.when(i == 0)
    def _():
        carry_ref[...] = jnp.zeros_like(carry_ref)

    x3 = x_ref[...] + _dot(hc_ref[...], wo_ref[...])
    x3_ref[...] = x3
    h3 = _rms(x3, g_ref[...])
    h3_ref[...] = h3

    h_hi, h_mid, _ = _split3(h3)
    w_hi, w_mid, _ = _split3(wr_ref[...])
    logits = _dot(h_hi, w_hi) + (_dot(h_hi, w_mid) + _dot(h_mid, w_hi))

    lane = lax.broadcasted_iota(I32, (tm, LANES), 1)
    lane_f = lane.astype(F32)
    lg = jnp.where(lane < N_EXPERTS, logits, NEG)
    v1 = jnp.max(lg, axis=1, keepdims=True)
    i1 = jnp.min(jnp.where(lg == v1, lane_f, float(LANES)), axis=1, keepdims=True)
    lg2 = jnp.where(lane_f == i1, NEG, lg)
    v2 = jnp.max(lg2, axis=1, keepdims=True)
    i2 = jnp.min(jnp.where(lg2 == v2, lane_f, float(LANES)), axis=1, keepdims=True)
    e = jnp.exp(v2 - v1)
    g1 = 1.0 / (1.0 + e)
    g2 = e / (1.0 + e)

    oh1 = lane_f == i1
    oh2 = lane_f == i2
    oh = jnp.where(oh1 | oh2, 1.0, 0.0)
    r = lax.broadcasted_iota(I32, (tm, tm), 0)
    c = lax.broadcasted_iota(I32, (tm, tm), 1)
    strict = jnp.where(c < r, 1.0, 0.0).astype(BF16)
    pos = _dot(strict, oh.astype(BF16)) + carry_ref[0:1, :]
    rank1 = jnp.sum(jnp.where(oh1, pos, 0.0), axis=1, keepdims=True)
    rank2 = jnp.sum(jnp.where(oh2, pos, 0.0), axis=1, keepdims=True)
    total = carry_ref[0:1, :] + jnp.sum(oh, axis=0, keepdims=True)
    carry_ref[...] = jnp.broadcast_to(total, carry_ref.shape)
    cnt_ref[...] = jnp.broadcast_to(total, cnt_ref.shape)

    meta = jnp.zeros((tm, LANES), F32)
    for k, val in enumerate((i1, i2, g1, g2, rank1, rank2)):
        meta = jnp.where(lane == k, val, meta)
    meta_ref[...] = meta


def _l1_out_router(x, hc, wo, g, wr):
    t = x.shape[0]
    tm = TM_OUT1
    row = lambda n: pl.BlockSpec((tm, n), lambda i: (i, 0))
    return pl.pallas_call(
        _l1_out_router_kernel,
        grid=(t // tm,),
        in_specs=[row(D_MODEL), row(MLSTM_WIDTH), _const_spec(wo.shape), _const_spec(g.shape),
                  _const_spec(wr.shape)],
        out_specs=[row(D_MODEL), row(D_MODEL), row(LANES), pl.BlockSpec((SUBLANES, LANES), lambda i: (0, 0))],
        out_shape=[jax.ShapeDtypeStruct((t, D_MODEL), F32), jax.ShapeDtypeStruct((t, D_MODEL), F32),
                   jax.ShapeDtypeStruct((t, LANES), F32), jax.ShapeDtypeStruct((SUBLANES, LANES), F32)],
        scratch_shapes=[pltpu.VMEM((SUBLANES, LANES), F32)],
        compiler_params=_cparams(("arbitrary",), 32),
        name="l1_out_router",
    )(x, hc, wo, g, wr)


def _row_copy(src_hbm, dst_hbm, r_src, r_dst, sem):
    return pltpu.make_async_copy(src_hbm.at[pl.ds(r_src, 1), :], dst_hbm.at[pl.ds(r_dst, 1), :], sem)


def _experts_kernel(te_ref, nused_ref, nrows_ref, src_ref, dst_ref, h_hbm, wg_ref, wu_ref, wd_ref, y_hbm,
                    x_scr, xb_scr, acc_scr, o_scr, gsem, ssem):
    i = pl.program_id(0)
    j = pl.program_id(1)
    nf = pl.num_programs(1)
    tm = x_scr.shape[0]
    valid = i < nused_ref[0]
    nrows = nrows_ref[i]

    @pl.when(valid & (j == 0))
    def _():
        def start(r, _):
            _row_copy(h_hbm, x_scr, src_ref[0, r], r, gsem).start()
            return _
        lax.fori_loop(0, tm, start, None)

        def wait(r, _):
            _row_copy(h_hbm, x_scr, 0, r, gsem).wait()
            return _
        lax.fori_loop(0, tm, wait, None)
        xb_scr[...] = x_scr[...].astype(BF16)
        acc_scr[...] = jnp.zeros_like(acc_scr)

    @pl.when(valid)
    def _():
        xb = xb_scr[...]
        gate = _dot(xb, wg_ref[...])
        act = (gate * _sigmoid(gate) * _dot(xb, wu_ref[...])).astype(BF16)
        acc_scr[...] += _dot(act, wd_ref[...])

    @pl.when(valid & (j == nf - 1))
    def _():
        o_scr[...] = acc_scr[...]

        def start(r, _):
            _row_copy(o_scr, y_hbm, r, dst_ref[0, r], ssem).start()
            return _
        lax.fori_loop(0, nrows, start, None)

        def wait(r, _):
            _row_copy(o_scr, y_hbm, r, 0, ssem).wait()
            return _
        lax.fori_loop(0, nrows, wait, None)


def _experts(tile_expert, nused, nrows, src, dst, h3, wg, wu, wd, n_out_rows):
    nt = src.shape[0]
    nf = D_FF_EXPERT // TF_E

    def wcol(i, j, te, nu, nr):
        return (te[i], 0, jnp.where(i < nu[0], j, nf - 1))

    def wrow(i, j, te, nu, nr):
        return (te[i], jnp.where(i < nu[0], j, nf - 1), 0)

    smem_tile = pl.BlockSpec((None, 1, TM_E), lambda i, j, te, nu, nr: (i, 0, 0), memory_space=pltpu.SMEM)
    grid_spec = pltpu.PrefetchScalarGridSpec(
        num_scalar_prefetch=3,
        grid=(nt, nf),
        in_specs=[smem_tile, smem_tile, pl.BlockSpec(memory_space=pl.ANY),
                  pl.BlockSpec((None, D_MODEL, TF_E), wcol), pl.BlockSpec((None, D_MODEL, TF_E), wcol),
                  pl.BlockSpec((None, TF_E, D_MODEL), wrow)],
        out_specs=pl.BlockSpec(memory_space=pl.ANY),
        scratch_shapes=[pltpu.VMEM((TM_E, D_MODEL), F32), pltpu.VMEM((TM_E, D_MODEL), BF16),
                        pltpu.VMEM((TM_E, D_MODEL), F32), pltpu.VMEM((TM_E, D_MODEL), F32),
                        pltpu.SemaphoreType.DMA(()), pltpu.SemaphoreType.DMA(())],
    )
    return pl.pallas_call(
        _experts_kernel,
        grid_spec=grid_spec,
        out_shape=jax.ShapeDtypeStruct((n_out_rows, D_MODEL), F32),
        compiler_params=_cparams(("arbitrary", "arbitrary"), 48),
        name="experts",
    )(tile_expert, nused, nrows, src, dst, h3, wg, wu, wd)


def _combine_kernel(x_ref, y0_ref, y1_ref, meta_ref, g_ref, o_ref):
    meta = meta_ref[...]
    y = x_ref[...] + meta[:, 2:3] * y0_ref[...] + meta[:, 3:4] * y1_ref[...]
    o_ref[...] = _rms(y, g_ref[...])


def _combine(x3, y, meta, g):
    t = x3.shape[0]
    tm = TM_FIN
    nt = t // tm
    row = lambda n: pl.BlockSpec((tm, n), lambda i: (i, 0))
    return pl.pallas_call(
        _combine_kernel,
        grid=(nt,),
        in_specs=[row(D_MODEL), row(D_MODEL), pl.BlockSpec((tm, D_MODEL), lambda i: (nt + i, 0)),
                  row(LANES), _const_spec(g.shape)],
        out_specs=row(D_MODEL),
        out_shape=jax.ShapeDtypeStruct((t, D_MODEL), F32),
        compiler_params=_cparams(("arbitrary",), 32),
        name="combine_final_norm",
    )(x3, y, y, meta, g)


def _route(meta, cnt, t):
    idx = meta[:, 0:2].astype(I32)
    rank = meta[:, 4:6].astype(I32)
    counts = cnt[0, :N_EXPERTS].astype(I32)
    tiles_e = (counts + TM_E - 1) // TM_E
    tile_end = jnp.cumsum(tiles_e)
    row_off = (tile_end - tiles_e) * TM_E
    nt = 2 * t // TM_E + N_EXPERTS
    p = nt * TM_E
    pos = (row_off[idx] + rank).reshape(-1)
    assign = jnp.zeros((p,), I32).at[pos].set(jnp.arange(2 * t, dtype=I32), unique_indices=True)
    src = assign >> 1
    dst = (assign & 1) * t + src
    tiles = jnp.arange(nt, dtype=I32)
    tile_expert = jnp.minimum(jnp.sum((tiles[:, None] >= tile_end[None, :]).astype(I32), axis=1),
                              N_EXPERTS - 1)
    first_tile = (tile_end - tiles_e)[tile_expert]
    nrows = jnp.clip(counts[tile_expert] - (tiles - first_tile) * TM_E, 0, TM_E)
    nused = tile_end[-1:].astype(I32)
    nrows = jnp.where(tiles < nused[0], nrows, 0).astype(I32)
    return tile_expert, nused, nrows, src.reshape(nt, 1, TM_E), dst.reshape(nt, 1, TM_E), 2 * t


def kernel(x, l0_norm1_g, l0_w_in, l0_b_f, l0_dw_w, l0_dw_b, l0_cln_g, l0_cln_b, l0_w_out, l0_norm2_g, l0_w_gate, l0_w_up, l0_w_down, l1_norm1_g, l1_w_in, l1_b_i, l1_b_f, l1_conv_w, l1_conv_b, l1_wq_head, l1_wk_head, l1_hnorm_g, l1_w_out, l1_norm2_g, l1_router, l1_e_gate, l1_e_up, l1_e_down, final_norm_g):
    batch, seq, d = x.shape
    t = batch * seq
    xf = x.reshape(t, d)
    vec = lambda a: a.reshape(1, -1).astype(F32)
    bf = lambda a: a.astype(BF16)

    fw = FOX_WIDTH
    o_f = 3 * fw
    o_a = o_f + FOX_HEADS
    o_g = o_a + CONV_CHANNELS
    w_qv = jnp.concatenate([l0_w_in[:, :fw], l0_w_in[:, 2 * fw:o_f]], axis=1)
    q, kta, v, u = _l0_in(
        xf, vec(l0_norm1_g), bf(w_qv), bf(l0_w_in[:, fw:2 * fw].T), bf(l0_w_in[:, o_f:o_a].T),
        l0_b_f.reshape(-1, 1).astype(F32), bf(l0_w_in[:, o_a:o_g]), bf(l0_w_in[:, o_g:]), batch=batch)
    att = _fox_attention(q, kta, v, batch=batch)
    dw_w = jnp.concatenate([l0_dw_w, jnp.zeros((1, CONV_CHANNELS), F32)], axis=0)
    uc = _conv_module(u, dw_w, vec(l0_dw_b), vec(l0_cln_g), vec(l0_cln_b), batch=batch)
    x2 = _l0_out_ffn(xf, att, uc, bf(l0_w_out[:fw]), bf(l0_w_out[fw:]), vec(l0_norm2_g),
                     bf(l0_w_gate), bf(l0_w_up), bf(l0_w_down))

    w = MLSTM_WIDTH
    nh = MLSTM_HEADS
    w_i = l1_w_in[:, 3 * w:3 * w + nh]
    w_f = l1_w_in[:, 3 * w + nh:]
    wfc = bf(jnp.pad(w_f, ((0, 0), (0, LANES - nh))))
    bfc = jnp.pad(l1_b_f, (0, LANES - nh)).reshape(1, LANES).astype(F32)
    wgr = bf(jnp.concatenate([w_i.T, w_f.T], axis=0))
    bgr = jnp.concatenate([l1_b_i, l1_b_f]).reshape(-1, 1).astype(F32)
    ql, kt, vl, og, lfc, gr = _l1_in(
        x2, vec(l1_norm1_g), bf(l1_w_in[:, :3 * w]), wfc, bfc, wgr, bgr, l1_conv_w.astype(F32),
        vec(l1_conv_b), bf(l1_wq_head), bf(jnp.swapaxes(l1_wk_head, 1, 2)), batch=batch)
    hc = _mlstm(ql, kt, vl, og, lfc, gr, vec(l1_hnorm_g), batch=batch)
    wr = jnp.pad(l1_router.astype(F32), ((0, 0), (0, LANES - N_EXPERTS)))
    x3, h3, meta, cnt = _l1_out_router(x2, hc, bf(l1_w_out), vec(l1_norm2_g), wr)

    tile_expert, nused, nrows, src, dst, n_rows = _route(meta, cnt, t)
    y = _experts(tile_expert, nused, nrows, src, dst, h3, bf(l1_e_gate), bf(l1_e_up), bf(l1_e_down), n_rows)
    out = _combine(x3, y, meta, vec(final_norm_g))
    return out.reshape(batch, seq, d)
```

```python
import functools

import jax
import jax.numpy as jnp
from jax import lax
from jax.experimental import pallas as pl
from jax.experimental.pallas import tpu as pltpu

F32 = jnp.float32
BF16 = jnp.bfloat16
I32 = jnp.int32

EPS = 1e-6
NEG = -1e30
LOG2E = 1.4426950408889634

D_MODEL = 1024
FOX_HEADS = 8
FOX_HEAD_DIM = 64
FOX_WIDTH = FOX_HEADS * FOX_HEAD_DIM
CONV_CHANNELS = 512
CONV_WIDTH = 31
MLSTM_HEADS = 8
MLSTM_HEAD_DIM = 128
MLSTM_WIDTH = MLSTM_HEADS * MLSTM_HEAD_DIM
MLSTM_CONV_WIDTH = 4
D_FF = 2816
N_EXPERTS = 8
D_FF_EXPERT = 3584

LANES = 128
SUBLANES = 8
MIB = 1024 * 1024

TM_IN0 = 512
TQ = 512
FOX_KAUG = 256
TM_CONV = 512
CONV_ROWS = 64
CONV_HALO = 32
TM_FFN0 = 512
TM_IN1 = 512
L1_HALO = 8
CHUNK = 256
TM_OUT1 = 512
TM_E = 512
TF_E = 512
GATHER_ROWS_PER_STEP = 80
EXPERT_SRC_WIDTH = 640
TM_FIN = 512


def _cparams(sem, vmem_mib):
    return pltpu.CompilerParams(dimension_semantics=sem, vmem_limit_bytes=vmem_mib * MIB)


def _rms(x, g):
    return x * lax.rsqrt(jnp.mean(x * x, axis=-1, keepdims=True) + EPS) * g


def _sigmoid(x):
    return 1.0 / (1.0 + jnp.exp(-x))


def _log_sigmoid(x):
    return jnp.minimum(x, 0.0) - jnp.log(1.0 + jnp.exp(-jnp.abs(x)))


def _dot(a, b):
    return jnp.dot(a, b, preferred_element_type=F32)


def _dot_nt(a, b):
    return lax.dot_general(a, b, (((1,), (1,)), ((), ())), preferred_element_type=F32)


def _split3(x):
    hi = x.astype(BF16)
    r1 = x - hi.astype(F32)
    mid = r1.astype(BF16)
    lo = (r1 - mid.astype(F32)).astype(BF16)
    return hi, mid, lo


def _const_spec(shape):
    nd = len(shape)
    return pl.BlockSpec(shape, lambda *_: (0,) * nd, pipeline_mode=pl.Buffered(1))


def _l0_in_kernel(x_ref, g_ref, wqv_ref, wkt_ref, wft_ref, bf_ref, wa_ref, wg_ref,
                  q_ref, kta_ref, v_ref, u_ref, carry_ref, *, tiles_per_batch):
    i = pl.program_id(0)
    tm = x_ref.shape[0]
    hb = _rms(x_ref[...], g_ref[...]).astype(BF16)
    qv = _dot(hb, wqv_ref[...])
    q_ref[...] = (qv[:, :FOX_WIDTH] * (FOX_HEAD_DIM ** -0.5 * LOG2E)).astype(BF16)
    v_ref[...] = qv[:, FOX_WIDTH:].astype(BF16)
    kt = _dot_nt(wkt_ref[...], hb).astype(BF16)
    u_ref[...] = _dot(hb, wa_ref[...]) * _sigmoid(_dot(hb, wg_ref[...]))

    logf = _log_sigmoid(_dot_nt(wft_ref[...], hb) + bf_ref[...])
    lane = lax.broadcasted_iota(I32, logf.shape, 1)
    cum = logf
    shift = 1
    while shift < tm:
        cum = cum + jnp.where(lane >= shift, pltpu.roll(cum, shift, 1), 0.0)
        shift *= 2

    @pl.when(i % tiles_per_batch == 0)
    def _():
        carry_ref[...] = jnp.zeros_like(carry_ref)

    c = cum + carry_ref[:, 0:1]
    carry_ref[...] = jnp.broadcast_to(c[:, tm - 1:tm], carry_ref.shape)

    pieces = [p.astype(F32) for p in _split3(c * LOG2E)]
    sub = lax.broadcasted_iota(I32, (2 * SUBLANES, tm), 0)
    zeros_tail = jnp.zeros((FOX_KAUG - LANES - 2 * SUBLANES, tm), BF16)
    for p in range(FOX_HEADS // 2):
        ext = jnp.zeros((2 * SUBLANES, tm), F32)
        for hh in range(2):
            for n, piece in enumerate(pieces):
                ext = jnp.where(sub == 3 * hh + n, piece[2 * p + hh:2 * p + hh + 1, :], ext)
        kta_ref[p, 0:LANES, :] = kt[p * LANES:(p + 1) * LANES, :]
        kta_ref[p, LANES:LANES + 2 * SUBLANES, :] = ext.astype(BF16)
        kta_ref[p, LANES + 2 * SUBLANES:, :] = zeros_tail


def _l0_in(x, g, wqv, wkt, wft, bf, wa, wg, *, batch):
    t = x.shape[0]
    tm = TM_IN0
    tpb = (t // batch) // tm
    pairs = FOX_HEADS // 2
    row = lambda w: pl.BlockSpec((tm, w), lambda i: (i, 0))
    consts = [g, wqv, wkt, wft, bf, wa, wg]
    return pl.pallas_call(
        functools.partial(_l0_in_kernel, tiles_per_batch=tpb),
        grid=(t // tm,),
        in_specs=[row(D_MODEL)] + [_const_spec(a.shape) for a in consts],
        out_specs=[row(FOX_WIDTH), pl.BlockSpec((pairs, FOX_KAUG, tm), lambda i: (0, 0, i)),
                   row(FOX_WIDTH), row(CONV_CHANNELS)],
        out_shape=[jax.ShapeDtypeStruct((t, FOX_WIDTH), BF16),
                   jax.ShapeDtypeStruct((pairs, FOX_KAUG, t), BF16),
                   jax.ShapeDtypeStruct((t, FOX_WIDTH), BF16),
                   jax.ShapeDtypeStruct((t, CONV_CHANNELS), F32)],
        scratch_shapes=[pltpu.VMEM((FOX_HEADS, LANES), F32)],
        compiler_params=_cparams(("arbitrary",), 40),
        name="l0_in",
    )(x, *consts)


def _fox_kernel(q_ref, kta_ref, v_ref, o_ref, qa_scr, qb_scr, m_scr, acc_scr):
    qi = pl.program_id(2)
    tq = q_ref.shape[0]
    tk = tq
    lane = lax.broadcasted_iota(I32, (tq, LANES), 1)
    is_a = lane < FOX_HEAD_DIM
    lane_k = lax.broadcasted_iota(I32, (tk, LANES), 1)

    q2 = q_ref[...].astype(F32)
    ext_a = jnp.where(lane < 3, -1.0, 0.0)
    ext_b = jnp.where((lane >= 3) & (lane < 6), -1.0, 0.0)
    qa_scr[...] = jnp.concatenate([jnp.where(is_a, q2, 0.0), ext_a], axis=1).astype(BF16)
    qb_scr[...] = jnp.concatenate([jnp.where(is_a, 0.0, q2), ext_b], axis=1).astype(BF16)
    m_scr[...] = jnp.full(m_scr.shape, NEG, F32)
    acc_scr[...] = jnp.zeros_like(acc_scr)

    def tile(ki, masked):
        start = pl.multiple_of(ki * tk, tk)
        kt = kta_ref[:, pl.ds(start, tk)]
        vf = v_ref[pl.ds(start, tk), :].astype(F32)
        v_augs = (jnp.where(lane_k < FOX_HEAD_DIM, vf, 1.0).astype(BF16),
                  jnp.where(lane_k < FOX_HEAD_DIM, 1.0, vf).astype(BF16))
        if masked:
            causal = (lax.broadcasted_iota(I32, (tq, tk), 1) <= lax.broadcasted_iota(I32, (tq, tk), 0))
        for h, q_scr in enumerate((qa_scr, qb_scr)):
            s = _dot(q_scr[...], kt)
            if masked:
                s = jnp.where(causal, s, NEG)
            m_old = m_scr[h]
            m_new = jnp.maximum(m_old, jnp.max(s, axis=1, keepdims=True))
            p = jnp.concatenate(
                [jnp.exp2(s[:, j * LANES:(j + 1) * LANES] - m_new).astype(BF16) for j in range(tk // LANES)],
                axis=1)
            acc_scr[h] = acc_scr[h] * jnp.exp2(m_old - m_new) + _dot(p, v_augs[h])
            m_scr[h] = m_new

    def body(ki, carry):
        tile(ki, False)
        return carry

    lax.fori_loop(0, qi, body, 0)
    tile(qi, True)

    acc_a = acc_scr[0]
    acc_b = acc_scr[1]
    o = jnp.where(is_a, acc_a / pltpu.roll(acc_a, FOX_HEAD_DIM, 1), acc_b / pltpu.roll(acc_b, FOX_HEAD_DIM, 1))
    o_ref[...] = o.astype(o_ref.dtype)


def _fox_attention(q, kta, v, *, batch):
    t = q.shape[0]
    seq = t // batch
    nq = seq // TQ
    pairs = FOX_HEADS // 2
    return pl.pallas_call(
        _fox_kernel,
        grid=(batch, pairs, nq),
        in_specs=[pl.BlockSpec((TQ, LANES), lambda b, p, i: (b * nq + i, p)),
                  pl.BlockSpec((None, FOX_KAUG, seq), lambda b, p, i: (p, 0, b)),
                  pl.BlockSpec((seq, LANES), lambda b, p, i: (b, p))],
        out_specs=pl.BlockSpec((TQ, LANES), lambda b, p, i: (b * nq + i, p)),
        out_shape=jax.ShapeDtypeStruct((t, FOX_WIDTH), BF16),
        scratch_shapes=[pltpu.VMEM((TQ, FOX_KAUG), BF16), pltpu.VMEM((TQ, FOX_KAUG), BF16),
                        pltpu.VMEM((2, TQ, LANES), F32), pltpu.VMEM((2, TQ, LANES), F32)],
        compiler_params=_cparams(("arbitrary", "arbitrary", "arbitrary"), 40),
        name="fox_attention",
    )(q, kta, v)


def _conv_mod_kernel(u_ref, w_ref, b_ref, g_ref, beta_ref, o_ref, win_ref, *, tiles_per_batch):
    i = pl.program_id(0)
    tm = u_ref.shape[0]

    @pl.when(i % tiles_per_batch == 0)
    def _():
        win_ref[0:CONV_HALO, :] = jnp.zeros((CONV_HALO, CONV_CHANNELS), F32)

    win_ref[CONV_HALO:CONV_HALO + tm, :] = u_ref[...]
    first = CONV_HALO - (CONV_WIDTH - 1)
    for c in range(tm // CONV_ROWS):
        r0 = c * CONV_ROWS
        acc = jnp.broadcast_to(b_ref[...], (CONV_ROWS, CONV_CHANNELS))
        for j in range(CONV_WIDTH):
            acc = acc + win_ref[r0 + first + j:r0 + first + j + CONV_ROWS, :] * w_ref[j:j + 1, :]
        mu = jnp.mean(acc, axis=-1, keepdims=True)
        d = acc - mu
        var = jnp.mean(d * d, axis=-1, keepdims=True)
        y = d * lax.rsqrt(var + EPS) * g_ref[...] + beta_ref[...]
        o_ref[r0:r0 + CONV_ROWS, :] = (y * _sigmoid(y)).astype(o_ref.dtype)
    win_ref[0:CONV_HALO, :] = win_ref[tm:tm + CONV_HALO, :]


def _conv_module(u, w, b, g, beta, *, batch):
    t = u.shape[0]
    tm = TM_CONV
    tpb = (t // batch) // tm
    return pl.pallas_call(
        functools.partial(_conv_mod_kernel, tiles_per_batch=tpb),
        grid=(t // tm,),
        in_specs=[pl.BlockSpec((tm, CONV_CHANNELS), lambda i: (i, 0)), _const_spec(w.shape),
                  _const_spec(b.shape), _const_spec(g.shape), _const_spec(beta.shape)],
        out_specs=pl.BlockSpec((tm, CONV_CHANNELS), lambda i: (i, 0)),
        out_shape=jax.ShapeDtypeStruct((t, CONV_CHANNELS), BF16),
        scratch_shapes=[pltpu.VMEM((CONV_HALO + tm, CONV_CHANNELS), F32)],
        compiler_params=_cparams(("arbitrary",), 32),
        name="conv_module",
    )(u, w, b, g, beta)


def _l0_out_ffn_kernel(x_ref, att_ref, u_ref, woa_ref, wou_ref, g_ref, wg_ref, wu_ref, wd_ref, o_ref):
    x1 = x_ref[...] + _dot(att_ref[...], woa_ref[...]) + _dot(u_ref[...], wou_ref[...])
    hb = _rms(x1, g_ref[...]).astype(BF16)
    gate = _dot(hb, wg_ref[...])
    act = (gate * _sigmoid(gate) * _dot(hb, wu_ref[...])).astype(BF16)
    o_ref[...] = x1 + _dot(act, wd_ref[...])


def _l0_out_ffn(x, att, u, woa, wou, g, wg, wu, wd):
    t = x.shape[0]
    tm = TM_FFN0
    row = lambda w: pl.BlockSpec((tm, w), lambda i: (i, 0))
    return pl.pallas_call(
        _l0_out_ffn_kernel,
        grid=(t // tm,),
        in_specs=[row(D_MODEL), row(FOX_WIDTH), row(CONV_CHANNELS), _const_spec(woa.shape),
                  _const_spec(wou.shape), _const_spec(g.shape), _const_spec(wg.shape),
                  _const_spec(wu.shape), _const_spec(wd.shape)],
        out_specs=row(D_MODEL),
        out_shape=jax.ShapeDtypeStruct((t, D_MODEL), F32),
        compiler_params=_cparams(("arbitrary",), 56),
        name="l0_out_ffn",
    )(x, att, u, woa, wou, g, wg, wu, wd)


def _l1_in_kernel(x_ref, g_ref, wuvo_ref, wfc_ref, bfc_ref, wgr_ref, bgr_ref, cw_ref, cb_ref,
                  wq_ref, wkt_ref, q_ref, kt_ref, v_ref, og_ref, lfc_ref, gr_ref, win_ref,
                  *, tiles_per_batch):
    i = pl.program_id(0)
    tm = x_ref.shape[0]
    w = MLSTM_WIDTH
    hb = _rms(x_ref[...], g_ref[...]).astype(BF16)
    uvo = _dot(hb, wuvo_ref[...])
    v_ref[...] = uvo[:, w:2 * w].astype(BF16)
    og_ref[...] = _sigmoid(uvo[:, 2 * w:]).astype(BF16)

    lfc_ref[...] = _log_sigmoid(_dot(hb, wfc_ref[...]) + bfc_ref[...])
    grow = _dot_nt(wgr_ref[...], hb) + bgr_ref[...]
    is_i = lax.broadcasted_iota(I32, grow.shape, 0) < MLSTM_HEADS
    gr_ref[...] = jnp.where(is_i, grow, _log_sigmoid(grow))

    @pl.when(i % tiles_per_batch == 0)
    def _():
        win_ref[0:L1_HALO, :] = jnp.zeros((L1_HALO, w), F32)

    win_ref[L1_HALO:L1_HALO + tm, :] = uvo[:, :w]
    first = L1_HALO - (MLSTM_CONV_WIDTH - 1)
    acc = jnp.broadcast_to(cb_ref[...], (tm, w))
    for j in range(MLSTM_CONV_WIDTH):
        acc = acc + win_ref[first + j:first + j + tm, :] * cw_ref[j:j + 1, :]
    win_ref[0:L1_HALO, :] = win_ref[tm:tm + L1_HALO, :]
    uc = (acc * _sigmoid(acc)).astype(BF16)
    d = MLSTM_HEAD_DIM
    for h in range(MLSTM_HEADS):
        uh = uc[:, h * d:(h + 1) * d]
        q_ref[:, h * d:(h + 1) * d] = _dot(uh, wq_ref[h]).astype(BF16)
        kt_ref[h * d:(h + 1) * d, :] = (_dot_nt(wkt_ref[h], uh) * (d ** -0.5)).astype(BF16)


def _l1_in(x, g, wuvo, wfc, bfc, wgr, bgr, cw, cb, wq, wkt, *, batch):
    t = x.shape[0]
    tm = TM_IN1
    tpb = (t // batch) // tm
    w = MLSTM_WIDTH
    row = lambda n: pl.BlockSpec((tm, n), lambda i: (i, 0))
    col = lambda n: pl.BlockSpec((n, tm), lambda i: (0, i))
    consts = [g, wuvo, wfc, bfc, wgr, bgr, cw, cb, wq, wkt]
    return pl.pallas_call(
        functools.partial(_l1_in_kernel, tiles_per_batch=tpb),
        grid=(t // tm,),
        in_specs=[row(D_MODEL)] + [_const_spec(a.shape) for a in consts],
        out_specs=[row(w), col(w), row(w), row(w), row(LANES), col(2 * MLSTM_HEADS)],
        out_shape=[jax.ShapeDtypeStruct((t, w), BF16), jax.ShapeDtypeStruct((w, t), BF16),
                   jax.ShapeDtypeStruct((t, w), BF16), jax.ShapeDtypeStruct((t, w), BF16),
                   jax.ShapeDtypeStruct((t, LANES), F32),
                   jax.ShapeDtypeStruct((2 * MLSTM_HEADS, t), F32)],
        scratch_shapes=[pltpu.VMEM((L1_HALO + tm, w), F32)],
        compiler_params=_cparams(("arbitrary",), 48),
        name="l1_in",
    )(x, *consts)


def _mlstm_kernel(q_ref, kt_ref, v_ref, og_ref, lfc_ref, gr_ref, hg_ref, o_ref, c_scr, m_scr):
    ci = pl.program_id(1)
    L = q_ref.shape[0]
    d = MLSTM_HEAD_DIM
    nh = MLSTM_HEADS

    @pl.when(ci == 0)
    def _():
        c_scr[...] = jnp.zeros_like(c_scr)
        m_scr[...] = jnp.zeros_like(m_scr)

    r = lax.broadcasted_iota(I32, (L, L), 0)
    c = lax.broadcasted_iota(I32, (L, L), 1)
    tril = r >= c
    tri_lo = jnp.where(tril, 1.0, 0.0).astype(BF16)
    tri_up = jnp.where(r <= c, 1.0, 0.0).astype(BF16)

    bc_col = sum(_dot(tri_lo, p) for p in _split3(lfc_ref[...]))
    bc_row = sum(_dot(p, tri_up) for p in _split3(gr_ref[nh:2 * nh, :]))
    ones_blk = jnp.ones((L, d), BF16)

    for h in range(nh):
        sl = slice(h * d, (h + 1) * d)
        qh = q_ref[:, sl]
        kth = kt_ref[sl, :]
        v_aug = jnp.concatenate([v_ref[:, sl], ones_blk], axis=1)
        bcol = bc_col[:, h:h + 1]
        brow = bc_row[h:h + 1, :]
        srow = gr_ref[h:h + 1, :] - brow
        g = brow[:, L - 1:L]
        m_old = m_scr[h][:, 0:1]
        ct = c_scr[h]

        dmat = jnp.where(tril, bcol + srow, NEG)
        inter = bcol + m_old
        m_t = jnp.maximum(inter, jnp.max(dmat, axis=1, keepdims=True))
        w_inter = jnp.exp(inter - m_t)
        w_intra = jnp.exp(dmat - m_t) * _dot(qh, kth)
        res = w_inter * _dot(qh, ct.astype(BF16)) + _dot(w_intra.astype(BF16), v_aug)
        den = jnp.maximum(jnp.abs(res[:, d:]), jnp.exp(-m_t))
        hc = og_ref[:, sl].astype(F32) * (res[:, :d] / den)
        o_ref[:, sl] = _rms(hc, hg_ref[:, sl]).astype(o_ref.dtype)

        a_row = g + srow
        m_new = jnp.maximum(g + m_old, jnp.max(a_row, axis=1, keepdims=True))
        w_in = jnp.exp(a_row - m_new)
        kw = (kth.astype(F32) * w_in).astype(BF16)
        c_scr[h] = jnp.exp(g + m_old - m_new) * ct + _dot(kw, v_aug)
        m_scr[h] = jnp.broadcast_to(m_new, (1, LANES))


def _mlstm(q, kt, v, og, lfc, gr, hg, *, batch):
    t = q.shape[0]
    w = MLSTM_WIDTH
    nc = (t // batch) // CHUNK
    row = lambda n: pl.BlockSpec((CHUNK, n), lambda b, c: (b * nc + c, 0))
    col = lambda n: pl.BlockSpec((n, CHUNK), lambda b, c: (0, b * nc + c))
    return pl.pallas_call(
        _mlstm_kernel,
        grid=(batch, nc),
        in_specs=[row(w), col(w), row(w), row(w), row(LANES), col(2 * MLSTM_HEADS), _const_spec(hg.shape)],
        out_specs=row(w),
        out_shape=jax.ShapeDtypeStruct((t, w), BF16),
        scratch_shapes=[pltpu.VMEM((MLSTM_HEADS, MLSTM_HEAD_DIM, 2 * MLSTM_HEAD_DIM), F32),
                        pltpu.VMEM((MLSTM_HEADS, 1, LANES), F32)],
        compiler_params=_cparams(("arbitrary", "arbitrary"), 32),
        name="mlstm",
    )(q, kt, v, og, lfc, gr, hg)


def _l1_out_router_kernel(x_ref, hc_ref, wo_ref, g_ref, wr_ref, x3_ref, h3_ref, meta_ref, cnt_ref, carry_ref):
    i = pl.program_id(0)
    tm = x_ref.shape[0]

    @pl.when(i == 0)
    def _():
        carry_ref[...] = jnp.zeros_like(carry_ref)

    x3 = x_ref[...] + _dot(hc_ref[...], wo_ref[...])
    x3_ref[...] = x3
    h3 = _rms(x3, g_ref[...])
    h3_ref[...] = h3

    h_hi, h_mid, _ = _split3(h3)
    w_hi, w_mid, _ = _split3(wr_ref[...])
    logits = _dot(h_hi, w_hi) + (_dot(h_hi, w_mid) + _dot(h_mid, w_hi))

    lane = lax.broadcasted_iota(I32, (tm, LANES), 1)
    lane_f = lane.astype(F32)
    lg = jnp.where(lane < N_EXPERTS, logits, NEG)
    v1 = jnp.max(lg, axis=1, keepdims=True)
    i1 = jnp.min(jnp.where(lg == v1, lane_f, float(LANES)), axis=1, keepdims=True)
    lg2 = jnp.where(lane_f == i1, NEG, lg)
    v2 = jnp.max(lg2, axis=1, keepdims=True)
    i2 = jnp.min(jnp.where(lg2 == v2, lane_f, float(LANES)), axis=1, keepdims=True)
    e = jnp.exp(v2 - v1)
    g1 = 1.0 / (1.0 + e)
    g2 = e / (1.0 + e)

    oh1 = lane_f == i1
    oh2 = lane_f == i2
    oh = jnp.where(oh1 | oh2, 1.0, 0.0)
    r = lax.broadcasted_iota(I32, (tm, tm), 0)
    c = lax.broadcasted_iota(I32, (tm, tm), 1)
    strict = jnp.where(c < r, 1.0, 0.0).astype(BF16)
    pos = _dot(strict, oh.astype(BF16)) + carry_ref[0:1, :]
    rank1 = jnp.sum(jnp.where(oh1, pos, 0.0), axis=1, keepdims=True)
    rank2 = jnp.sum(jnp.where(oh2, pos, 0.0), axis=1, keepdims=True)
    total = carry_ref[0:1, :] + jnp.sum(oh, axis=0, keepdims=True)
    carry_ref[...] = jnp.broadcast_to(total, carry_ref.shape)
    cnt_ref[...] = jnp.broadcast_to(total, cnt_ref.shape)

    meta = jnp.zeros((tm, LANES), F32)
    for k, val in enumerate((i1, i2, g1, g2, rank1, rank2)):
        meta = jnp.where(lane == k, val, meta)
    meta_ref[...] = meta


def _l1_out_router(x, hc, wo, g, wr):
    t = x.shape[0]
    tm = TM_OUT1
    row = lambda n: pl.BlockSpec((tm, n), lambda i: (i, 0))
    return pl.pallas_call(
        _l1_out_router_kernel,
        grid=(t // tm,),
        in_specs=[row(D_MODEL), row(MLSTM_WIDTH), _const_spec(wo.shape), _const_spec(g.shape),
                  _const_spec(wr.shape)],
        out_specs=[row(D_MODEL), row(D_MODEL), row(LANES), pl.BlockSpec((SUBLANES, LANES), lambda i: (0, 0))],
        out_shape=[jax.ShapeDtypeStruct((t, D_MODEL), F32), jax.ShapeDtypeStruct((t, D_MODEL), F32),
                   jax.ShapeDtypeStruct((t, LANES), F32), jax.ShapeDtypeStruct((SUBLANES, LANES), F32)],
        scratch_shapes=[pltpu.VMEM((SUBLANES, LANES), F32)],
        compiler_params=_cparams(("arbitrary",), 32),
        name="l1_out_router",
    )(x, hc, wo, g, wr)


def _gather_row(h_hbm, x_scr, sem, slot, tok, r):
    return pltpu.make_async_copy(h_hbm.at[pl.ds(tok, 1), :], x_scr.at[slot, pl.ds(r, 1), :], sem.at[slot])


def _experts_kernel(te_ref, nused_ref, src0_ref, srcn_ref, h_hbm, wg_ref, wu_ref, wd_ref, o_ref,
                    x_scr, xb_scr, gsem):
    i = pl.program_id(0)
    j = pl.program_id(1)
    nt = pl.num_programs(0)
    nf = pl.num_programs(1)
    tm = o_ref.shape[0]
    rows = x_scr.shape[1]
    valid = i < nused_ref[0]
    slot = i % 2

    def wait_tile(s):
        def wait(r, c):
            _gather_row(h_hbm, x_scr, gsem, s, 0, r).wait()
            return c
        lax.fori_loop(0, rows, wait, 0, unroll=8)

    def prefetch_next():
        for k in range(GATHER_ROWS_PER_STEP):
            r = j * GATHER_ROWS_PER_STEP + k
            _gather_row(h_hbm, x_scr, gsem, 1 - slot, srcn_ref[0, r], r).start()

    @pl.when((i == 0) & (j == 0))
    def _():
        def start(r, c):
            _gather_row(h_hbm, x_scr, gsem, 0, src0_ref[0, r], r).start()
            return c
        lax.fori_loop(0, rows, start, 0, unroll=8)

    @pl.when(j == 0)
    def _():
        wait_tile(slot)
        xb_scr[...] = x_scr[slot, 0:tm, :].astype(BF16)
        o_ref[...] = jnp.zeros_like(o_ref)

    @pl.when(valid)
    def _():
        xb = xb_scr[...]
        gate = _dot(xb, wg_ref[...])
        up = _dot(xb, wu_ref[...])
        prefetch_next()
        act = (gate * _sigmoid(gate) * up).astype(BF16)
        o_ref[...] += _dot(act, wd_ref[...])

    @pl.when(jnp.logical_not(valid))
    def _():
        prefetch_next()

    @pl.when((i == nt - 1) & (j == nf - 1))
    def _():
        wait_tile(1 - slot)


def _experts(tile_expert, nused, src, h3, wg, wu, wd):
    nt = src.shape[0] - 1
    nf = D_FF_EXPERT // TF_E
    rows = nf * GATHER_ROWS_PER_STEP

    def wcol(i, j, te, nu):
        return (te[i], 0, jnp.where(i < nu[0], j, nf - 1))

    def wrow(i, j, te, nu):
        return (te[i], jnp.where(i < nu[0], j, nf - 1), 0)

    def smem_tile(index_map):
        return pl.BlockSpec((None, 1, src.shape[2]), index_map, memory_space=pltpu.SMEM)

    grid_spec = pltpu.PrefetchScalarGridSpec(
        num_scalar_prefetch=2,
        grid=(nt, nf),
        in_specs=[smem_tile(lambda i, j, te, nu: (0, 0, 0)), smem_tile(lambda i, j, te, nu: (i + 1, 0, 0)),
                  pl.BlockSpec(memory_space=pl.ANY),
                  pl.BlockSpec((None, D_MODEL, TF_E), wcol), pl.BlockSpec((None, D_MODEL, TF_E), wcol),
                  pl.BlockSpec((None, TF_E, D_MODEL), wrow)],
        out_specs=pl.BlockSpec((TM_E, D_MODEL), lambda i, j, te, nu: (i, 0)),
        scratch_shapes=[pltpu.VMEM((2, rows, D_MODEL), F32), pltpu.VMEM((TM_E, D_MODEL), BF16),
                        pltpu.SemaphoreType.DMA((2,))],
    )
    return pl.pallas_call(
        _experts_kernel,
        grid_spec=grid_spec,
        out_shape=jax.ShapeDtypeStruct((nt * TM_E, D_MODEL), F32),
        compiler_params=_cparams(("arbitrary", "arbitrary"), 48),
        name="experts",
    )(tile_expert, nused, src, src, h3, wg, wu, wd)


def _combine_kernel(pos0_ref, posn_ref, x_ref, meta_ref, g_ref, y_hbm, o_ref, y_scr, sem):
    i = pl.program_id(0)
    n = pl.num_programs(0)
    tm = x_ref.shape[0]
    slot = i % 2

    def row_copy(s, pos, r):
        return pltpu.make_async_copy(y_hbm.at[pl.ds(pos, 1), :], y_scr.at[s, pl.ds(r, 1), :], sem.at[s])

    def start_tile(pos_ref, s):
        def start(r, c):
            row_copy(s, pos_ref[0, r], r).start()
            return c
        lax.fori_loop(0, 2 * tm, start, 0, unroll=8)

    @pl.when(i == 0)
    def _():
        start_tile(pos0_ref, 0)

    @pl.when(i + 1 < n)
    def _():
        start_tile(posn_ref, 1 - slot)

    def wait(r, c):
        row_copy(slot, 0, r).wait()
        return c
    lax.fori_loop(0, 2 * tm, wait, 0, unroll=8)

    meta = meta_ref[...]
    y = x_ref[...] + meta[:, 2:3] * y_scr[slot, 0:tm, :] + meta[:, 3:4] * y_scr[slot, tm:2 * tm, :]
    o_ref[...] = _rms(y, g_ref[...])


def _combine(pos, x3, y, meta, g):
    t = x3.shape[0]
    tm = TM_FIN
    nt = t // tm
    row = lambda n: pl.BlockSpec((tm, n), lambda i: (i, 0))

    def smem_tile(index_map):
        return pl.BlockSpec((None, 1, 2 * tm), index_map, memory_space=pltpu.SMEM)

    return pl.pallas_call(
        _combine_kernel,
        grid=(nt,),
        in_specs=[smem_tile(lambda i: (0, 0, 0)), smem_tile(lambda i: (jnp.minimum(i + 1, nt - 1), 0, 0)),
                  row(D_MODEL), row(LANES), _const_spec(g.shape), pl.BlockSpec(memory_space=pl.ANY)],
        out_specs=row(D_MODEL),
        out_shape=jax.ShapeDtypeStruct((t, D_MODEL), F32),
        scratch_shapes=[pltpu.VMEM((2, 2 * tm, D_MODEL), F32), pltpu.SemaphoreType.DMA((2,))],
        compiler_params=_cparams(("arbitrary",), 40),
        name="combine_final_norm",
    )(pos, pos, x3, meta, g, y)


def _route(meta, cnt, t):
    idx = meta[:, 0:2].astype(I32)
    rank = meta[:, 4:6].astype(I32)
    counts = cnt[0, :N_EXPERTS].astype(I32)
    tiles_e = (counts + TM_E - 1) // TM_E
    tile_end = jnp.cumsum(tiles_e)
    row_off = (tile_end - tiles_e) * TM_E
    nt = 2 * t // TM_E + N_EXPERTS
    p = nt * TM_E
    pos = row_off[idx] + rank
    src = jnp.zeros((p,), I32).at[pos.reshape(-1)].set(jnp.repeat(jnp.arange(t, dtype=I32), 2),
                                                       unique_indices=True)
    src = jnp.pad(src.reshape(nt, 1, TM_E), ((0, 1), (0, 0), (0, EXPERT_SRC_WIDTH - TM_E)))
    tiles = jnp.arange(nt, dtype=I32)
    tile_expert = jnp.minimum(jnp.sum((tiles[:, None] >= tile_end[None, :]).astype(I32), axis=1),
                              N_EXPERTS - 1)
    nused = tile_end[-1:].astype(I32)
    pos_tiles = pos.reshape(t // TM_FIN, TM_FIN, 2).transpose(0, 2, 1).reshape(t // TM_FIN, 1, 2 * TM_FIN)
    return tile_expert, nused, src, pos_tiles


def kernel(x, l0_norm1_g, l0_w_in, l0_b_f, l0_dw_w, l0_dw_b, l0_cln_g, l0_cln_b, l0_w_out, l0_norm2_g, l0_w_gate, l0_w_up, l0_w_down, l1_norm1_g, l1_w_in, l1_b_i, l1_b_f, l1_conv_w, l1_conv_b, l1_wq_head, l1_wk_head, l1_hnorm_g, l1_w_out, l1_norm2_g, l1_router, l1_e_gate, l1_e_up, l1_e_down, final_norm_g):
    batch, seq, d = x.shape
    t = batch * seq
    xf = x.reshape(t, d)
    vec = lambda a: a.reshape(1, -1).astype(F32)
    bf = lambda a: a.astype(BF16)

    fw = FOX_WIDTH
    o_f = 3 * fw
    o_a = o_f + FOX_HEADS
    o_g = o_a + CONV_CHANNELS
    w_qv = jnp.concatenate([l0_w_in[:, :fw], l0_w_in[:, 2 * fw:o_f]], axis=1)
    q, kta, v, u = _l0_in(
        xf, vec(l0_norm1_g), bf(w_qv), bf(l0_w_in[:, fw:2 * fw].T), bf(l0_w_in[:, o_f:o_a].T),
        l0_b_f.reshape(-1, 1).astype(F32), bf(l0_w_in[:, o_a:o_g]), bf(l0_w_in[:, o_g:]), batch=batch)
    att = _fox_attention(q, kta, v, batch=batch)
    dw_w = jnp.concatenate([l0_dw_w, jnp.zeros((1, CONV_CHANNELS), F32)], axis=0)
    uc = _conv_module(u, dw_w, vec(l0_dw_b), vec(l0_cln_g), vec(l0_cln_b), batch=batch)
    x2 = _l0_out_ffn(xf, att, uc, bf(l0_w_out[:fw]), bf(l0_w_out[fw:]), vec(l0_norm2_g),
                     bf(l0_w_gate), bf(l0_w_up), bf(l0_w_down))

    w = MLSTM_WIDTH
    nh = MLSTM_HEADS
    w_i = l1_w_in[:, 3 * w:3 * w + nh]
    w_f = l1_w_in[:, 3 * w + nh:]
    wfc = bf(jnp.pad(w_f, ((0, 0), (0, LANES - nh))))
    bfc = jnp.pad(l1_b_f, (0, LANES - nh)).reshape(1, LANES).astype(F32)
    wgr = bf(jnp.concatenate([w_i.T, w_f.T], axis=0))
    bgr = jnp.concatenate([l1_b_i, l1_b_f]).reshape(-1, 1).astype(F32)
    ql, kt, vl, og, lfc, gr = _l1_in(
        x2, vec(l1_norm1_g), bf(l1_w_in[:, :3 * w]), wfc, bfc, wgr, bgr, l1_conv_w.astype(F32),
        vec(l1_conv_b), bf(l1_wq_head), bf(jnp.swapaxes(l1_wk_head, 1, 2)), batch=batch)
    hc = _mlstm(ql, kt, vl, og, lfc, gr, vec(l1_hnorm_g), batch=batch)
    wr = jnp.pad(l1_router.astype(F32), ((0, 0), (0, LANES - N_EXPERTS)))
    x3, h3, meta, cnt = _l1_out_router(x2, hc, bf(l1_w_out), vec(l1_norm2_g), wr)

    tile_expert, nused, src, pos_tiles = _route(meta, cnt, t)
    y = _experts(tile_expert, nused, src, h3, bf(l1_e_gate), bf(l1_e_up), bf(l1_e_down))
    out = _combine(pos_tiles, x3, y, meta, vec(final_norm_g))
    return out.reshape(batch, seq, d)
```

```python
import functools

import jax
import jax.numpy as jnp
from jax import lax
from jax.experimental import pallas as pl
from jax.experimental.pallas import tpu as pltpu

F32 = jnp.float32
BF16 = jnp.bfloat16
I32 = jnp.int32

EPS = 1e-6
NEG = -1e30
LOG2E = 1.4426950408889634

D_MODEL = 1024
FOX_HEADS = 8
FOX_HEAD_DIM = 64
FOX_WIDTH = FOX_HEADS * FOX_HEAD_DIM
CONV_CHANNELS = 512
CONV_WIDTH = 31
MLSTM_HEADS = 8
MLSTM_HEAD_DIM = 128
MLSTM_WIDTH = MLSTM_HEADS * MLSTM_HEAD_DIM
MLSTM_CONV_WIDTH = 4
D_FF = 2816
N_EXPERTS = 8
D_FF_EXPERT = 3584

LANES = 128
SUBLANES = 8
MIB = 1024 * 1024

TM_IN0 = 512
TQ = 512
FOX_Q_HALVES = 2
FOX_KAUG = 256
TM_CONV = 512
CONV_ROWS = 64
CONV_HALO = 32
TM_FFN0 = 512
TM_IN1 = 512
L1_HALO = 8
CHUNK = 256
TM_OUT1 = 512
TM_E = 1024
TF_E = 512
GATHER_ROWS_PER_STEP = 160
EXPERT_SRC_WIDTH = 1152
TM_FIN = 512


def _cparams(sem, vmem_mib):
    return pltpu.CompilerParams(dimension_semantics=sem, vmem_limit_bytes=vmem_mib * MIB)


def _rms(x, g):
    return x * lax.rsqrt(jnp.mean(x * x, axis=-1, keepdims=True) + EPS) * g


def _sigmoid(x):
    return 1.0 / (1.0 + jnp.exp(-x))


def _log_sigmoid(x):
    return jnp.minimum(x, 0.0) - jnp.log(1.0 + jnp.exp(-jnp.abs(x)))


def _dot(a, b):
    return jnp.dot(a, b, preferred_element_type=F32)


def _dot_nt(a, b):
    return lax.dot_general(a, b, (((1,), (1,)), ((), ())), preferred_element_type=F32)


def _split3(x):
    hi = x.astype(BF16)
    r1 = x - hi.astype(F32)
    mid = r1.astype(BF16)
    lo = (r1 - mid.astype(F32)).astype(BF16)
    return hi, mid, lo


def _const_spec(shape):
    nd = len(shape)
    return pl.BlockSpec(shape, lambda *_: (0,) * nd, pipeline_mode=pl.Buffered(1))


def _l0_in_kernel(x_ref, g_ref, wqv_ref, wkt_ref, wft_ref, bf_ref, wa_ref, wg_ref,
                  q_ref, kta_ref, v_ref, u_ref, carry_ref, *, tiles_per_batch):
    i = pl.program_id(0)
    tm = x_ref.shape[0]
    hb = _rms(x_ref[...], g_ref[...]).astype(BF16)
    qv = _dot(hb, wqv_ref[...])
    q_ref[...] = (qv[:, :FOX_WIDTH] * (FOX_HEAD_DIM ** -0.5 * LOG2E)).astype(BF16)
    v_ref[...] = qv[:, FOX_WIDTH:].astype(BF16)
    kt = _dot_nt(wkt_ref[...], hb).astype(BF16)
    u_ref[...] = _dot(hb, wa_ref[...]) * _sigmoid(_dot(hb, wg_ref[...]))

    logf = _log_sigmoid(_dot_nt(wft_ref[...], hb) + bf_ref[...])
    lane = lax.broadcasted_iota(I32, logf.shape, 1)
    cum = logf
    shift = 1
    while shift < tm:
        cum = cum + jnp.where(lane >= shift, pltpu.roll(cum, shift, 1), 0.0)
        shift *= 2

    @pl.when(i % tiles_per_batch == 0)
    def _():
        carry_ref[...] = jnp.zeros_like(carry_ref)

    c = cum + carry_ref[:, 0:1]
    carry_ref[...] = jnp.broadcast_to(c[:, tm - 1:tm], carry_ref.shape)

    pieces = [p.astype(F32) for p in _split3(c * LOG2E)]
    sub = lax.broadcasted_iota(I32, (2 * SUBLANES, tm), 0)
    zeros_tail = jnp.zeros((FOX_KAUG - LANES - 2 * SUBLANES, tm), BF16)
    for p in range(FOX_HEADS // 2):
        ext = jnp.zeros((2 * SUBLANES, tm), F32)
        for hh in range(2):
            for n, piece in enumerate(pieces):
                ext = jnp.where(sub == 3 * hh + n, piece[2 * p + hh:2 * p + hh + 1, :], ext)
        kta_ref[p, 0:LANES, :] = kt[p * LANES:(p + 1) * LANES, :]
        kta_ref[p, LANES:LANES + 2 * SUBLANES, :] = ext.astype(BF16)
        kta_ref[p, LANES + 2 * SUBLANES:, :] = zeros_tail


def _l0_in(x, g, wqv, wkt, wft, bf, wa, wg, *, batch):
    t = x.shape[0]
    tm = TM_IN0
    tpb = (t // batch) // tm
    pairs = FOX_HEADS // 2
    row = lambda w: pl.BlockSpec((tm, w), lambda i: (i, 0))
    consts = [g, wqv, wkt, wft, bf, wa, wg]
    return pl.pallas_call(
        functools.partial(_l0_in_kernel, tiles_per_batch=tpb),
        grid=(t // tm,),
        in_specs=[row(D_MODEL)] + [_const_spec(a.shape) for a in consts],
        out_specs=[row(FOX_WIDTH), pl.BlockSpec((pairs, FOX_KAUG, tm), lambda i: (0, 0, i)),
                   row(FOX_WIDTH), row(CONV_CHANNELS)],
        out_shape=[jax.ShapeDtypeStruct((t, FOX_WIDTH), BF16),
                   jax.ShapeDtypeStruct((pairs, FOX_KAUG, t), BF16),
                   jax.ShapeDtypeStruct((t, FOX_WIDTH), BF16),
                   jax.ShapeDtypeStruct((t, CONV_CHANNELS), F32)],
        scratch_shapes=[pltpu.VMEM((FOX_HEADS, LANES), F32)],
        compiler_params=_cparams(("arbitrary",), 40),
        name="l0_in",
    )(x, *consts)


def _fox_kernel(q_ref, kta_ref, v_ref, o_ref, q_scr, m_scr, acc_scr):
    qi = pl.program_id(2)
    tq = TQ
    tk = TQ
    lane = lax.broadcasted_iota(I32, (tq, LANES), 1)
    is_a = lane < FOX_HEAD_DIM
    lane_k = lax.broadcasted_iota(I32, (tk, LANES), 1)

    ext_a = jnp.where(lane < 3, -1.0, 0.0)
    ext_b = jnp.where((lane >= 3) & (lane < 6), -1.0, 0.0)
    for half in range(FOX_Q_HALVES):
        q2 = q_ref[half * tq:(half + 1) * tq, :].astype(F32)
        q_scr[2 * half] = jnp.concatenate([jnp.where(is_a, q2, 0.0), ext_a], axis=1).astype(BF16)
        q_scr[2 * half + 1] = jnp.concatenate([jnp.where(is_a, 0.0, q2), ext_b], axis=1).astype(BF16)
    m_scr[...] = jnp.full(m_scr.shape, NEG, F32)
    acc_scr[...] = jnp.zeros_like(acc_scr)

    def tile(ki, plan):
        start = pl.multiple_of(ki * tk, tk)
        kt = kta_ref[:, pl.ds(start, tk)]
        vf = v_ref[pl.ds(start, tk), :].astype(F32)
        v_augs = (jnp.where(lane_k < FOX_HEAD_DIM, vf, 1.0).astype(BF16),
                  jnp.where(lane_k < FOX_HEAD_DIM, 1.0, vf).astype(BF16))
        causal = (lax.broadcasted_iota(I32, (tq, tk), 1) <= lax.broadcasted_iota(I32, (tq, tk), 0))
        for half, masked in plan:
            for h in range(2):
                n = 2 * half + h
                s = _dot(q_scr[n], kt)
                if masked:
                    s = jnp.where(causal, s, NEG)
                m_old = m_scr[n]
                m_new = jnp.maximum(m_old, jnp.max(s, axis=1, keepdims=True))
                p = jnp.concatenate(
                    [jnp.exp2(s[:, j * LANES:(j + 1) * LANES] - m_new).astype(BF16) for j in range(tk // LANES)],
                    axis=1)
                acc_scr[n] = acc_scr[n] * jnp.exp2(m_old - m_new) + _dot(p, v_augs[h])
                m_scr[n] = m_new

    def body(ki, carry):
        tile(ki, tuple((half, False) for half in range(FOX_Q_HALVES)))
        return carry

    first_diag = FOX_Q_HALVES * qi
    lax.fori_loop(0, first_diag, body, 0)
    for d in range(FOX_Q_HALVES):
        tile(first_diag + d, ((d, True),) + tuple((half, False) for half in range(d + 1, FOX_Q_HALVES)))

    for half in range(FOX_Q_HALVES):
        acc_a = acc_scr[2 * half]
        acc_b = acc_scr[2 * half + 1]
        o = jnp.where(is_a, acc_a / pltpu.roll(acc_a, FOX_HEAD_DIM, 1),
                      acc_b / pltpu.roll(acc_b, FOX_HEAD_DIM, 1))
        o_ref[half * tq:(half + 1) * tq, :] = o.astype(o_ref.dtype)


def _fox_attention(q, kta, v, *, batch):
    t = q.shape[0]
    seq = t // batch
    rows = FOX_Q_HALVES * TQ
    nq = seq // rows
    pairs = FOX_HEADS // 2
    nstate = 2 * FOX_Q_HALVES
    return pl.pallas_call(
        _fox_kernel,
        grid=(batch, pairs, nq),
        in_specs=[pl.BlockSpec((rows, LANES), lambda b, p, i: (b * nq + i, p)),
                  pl.BlockSpec((None, FOX_KAUG, seq), lambda b, p, i: (p, 0, b)),
                  pl.BlockSpec((seq, LANES), lambda b, p, i: (b, p))],
        out_specs=pl.BlockSpec((rows, LANES), lambda b, p, i: (b * nq + i, p)),
        out_shape=jax.ShapeDtypeStruct((t, FOX_WIDTH), BF16),
        scratch_shapes=[pltpu.VMEM((nstate, TQ, FOX_KAUG), BF16),
                        pltpu.VMEM((nstate, TQ, LANES), F32), pltpu.VMEM((nstate, TQ, LANES), F32)],
        compiler_params=_cparams(("arbitrary", "arbitrary", "arbitrary"), 40),
        name="fox_attention",
    )(q, kta, v)


def _conv_mod_kernel(u_ref, w_ref, b_ref, g_ref, beta_ref, o_ref, win_ref, *, tiles_per_batch):
    i = pl.program_id(0)
    tm = u_ref.shape[0]

    @pl.when(i % tiles_per_batch == 0)
    def _():
        win_ref[0:CONV_HALO, :] = jnp.zeros((CONV_HALO, CONV_CHANNELS), F32)

    win_ref[CONV_HALO:CONV_HALO + tm, :] = u_ref[...]
    first = CONV_HALO - (CONV_WIDTH - 1)
    for c in range(tm // CONV_ROWS):
        r0 = c * CONV_ROWS
        acc = jnp.broadcast_to(b_ref[...], (CONV_ROWS, CONV_CHANNELS))
        for j in range(CONV_WIDTH):
            acc = acc + win_ref[r0 + first + j:r0 + first + j + CONV_ROWS, :] * w_ref[j:j + 1, :]
        mu = jnp.mean(acc, axis=-1, keepdims=True)
        d = acc - mu
        var = jnp.mean(d * d, axis=-1, keepdims=True)
        y = d * lax.rsqrt(var + EPS) * g_ref[...] + beta_ref[...]
        o_ref[r0:r0 + CONV_ROWS, :] = (y * _sigmoid(y)).astype(o_ref.dtype)
    win_ref[0:CONV_HALO, :] = win_ref[tm:tm + CONV_HALO, :]


def _conv_module(u, w, b, g, beta, *, batch):
    t = u.shape[0]
    tm = TM_CONV
    tpb = (t // batch) // tm
    return pl.pallas_call(
        functools.partial(_conv_mod_kernel, tiles_per_batch=tpb),
        grid=(t // tm,),
        in_specs=[pl.BlockSpec((tm, CONV_CHANNELS), lambda i: (i, 0)), _const_spec(w.shape),
                  _const_spec(b.shape), _const_spec(g.shape), _const_spec(beta.shape)],
        out_specs=pl.BlockSpec((tm, CONV_CHANNELS), lambda i: (i, 0)),
        out_shape=jax.ShapeDtypeStruct((t, CONV_CHANNELS), BF16),
        scratch_shapes=[pltpu.VMEM((CONV_HALO + tm, CONV_CHANNELS), F32)],
        compiler_params=_cparams(("arbitrary",), 32),
        name="conv_module",
    )(u, w, b, g, beta)


def _l0_out_ffn_kernel(x_ref, att_ref, u_ref, woa_ref, wou_ref, g_ref, wg_ref, wu_ref, wd_ref, o_ref):
    x1 = x_ref[...] + _dot(att_ref[...], woa_ref[...]) + _dot(u_ref[...], wou_ref[...])
    hb = _rms(x1, g_ref[...]).astype(BF16)
    gate = _dot(hb, wg_ref[...])
    act = (gate * _sigmoid(gate) * _dot(hb, wu_ref[...])).astype(BF16)
    o_ref[...] = x1 + _dot(act, wd_ref[...])


def _l0_out_ffn(x, att, u, woa, wou, g, wg, wu, wd):
    t = x.shape[0]
    tm = TM_FFN0
    row = lambda w: pl.BlockSpec((tm, w), lambda i: (i, 0))
    return pl.pallas_call(
        _l0_out_ffn_kernel,
        grid=(t // tm,),
        in_specs=[row(D_MODEL), row(FOX_WIDTH), row(CONV_CHANNELS), _const_spec(woa.shape),
                  _const_spec(wou.shape), _const_spec(g.shape), _const_spec(wg.shape),
                  _const_spec(wu.shape), _const_spec(wd.shape)],
        out_specs=row(D_MODEL),
        out_shape=jax.ShapeDtypeStruct((t, D_MODEL), F32),
        compiler_params=_cparams(("arbitrary",), 56),
        name="l0_out_ffn",
    )(x, att, u, woa, wou, g, wg, wu, wd)


def _l1_in_kernel(x_ref, g_ref, wuvo_ref, wfc_ref, bfc_ref, wgr_ref, bgr_ref, cw_ref, cb_ref,
                  wq_ref, wkt_ref, q_ref, kt_ref, v_ref, og_ref, lfc_ref, gr_ref, win_ref,
                  *, tiles_per_batch):
    i = pl.program_id(0)
    tm = x_ref.shape[0]
    w = MLSTM_WIDTH
    hb = _rms(x_ref[...], g_ref[...]).astype(BF16)
    uvo = _dot(hb, wuvo_ref[...])
    v_ref[...] = uvo[:, w:2 * w].astype(BF16)
    og_ref[...] = _sigmoid(uvo[:, 2 * w:]).astype(BF16)

    lfc_ref[...] = _log_sigmoid(_dot(hb, wfc_ref[...]) + bfc_ref[...])
    grow = _dot_nt(wgr_ref[...], hb) + bgr_ref[...]
    is_i = lax.broadcasted_iota(I32, grow.shape, 0) < MLSTM_HEADS
    gr_ref[...] = jnp.where(is_i, grow, _log_sigmoid(grow))

    @pl.when(i % tiles_per_batch == 0)
    def _():
        win_ref[0:L1_HALO, :] = jnp.zeros((L1_HALO, w), F32)

    win_ref[L1_HALO:L1_HALO + tm, :] = uvo[:, :w]
    first = L1_HALO - (MLSTM_CONV_WIDTH - 1)
    acc = jnp.broadcast_to(cb_ref[...], (tm, w))
    for j in range(MLSTM_CONV_WIDTH):
        acc = acc + win_ref[first + j:first + j + tm, :] * cw_ref[j:j + 1, :]
    win_ref[0:L1_HALO, :] = win_ref[tm:tm + L1_HALO, :]
    uc = (acc * _sigmoid(acc)).astype(BF16)
    d = MLSTM_HEAD_DIM
    for h in range(MLSTM_HEADS):
        uh = uc[:, h * d:(h + 1) * d]
        q_ref[:, h * d:(h + 1) * d] = _dot(uh, wq_ref[h]).astype(BF16)
        kt_ref[h * d:(h + 1) * d, :] = (_dot_nt(wkt_ref[h], uh) * (d ** -0.5)).astype(BF16)


def _l1_in(x, g, wuvo, wfc, bfc, wgr, bgr, cw, cb, wq, wkt, *, batch):
    t = x.shape[0]
    tm = TM_IN1
    tpb = (t // batch) // tm
    w = MLSTM_WIDTH
    row = lambda n: pl.BlockSpec((tm, n), lambda i: (i, 0))
    col = lambda n: pl.BlockSpec((n, tm), lambda i: (0, i))
    consts = [g, wuvo, wfc, bfc, wgr, bgr, cw, cb, wq, wkt]
    return pl.pallas_call(
        functools.partial(_l1_in_kernel, tiles_per_batch=tpb),
        grid=(t // tm,),
        in_specs=[row(D_MODEL)] + [_const_spec(a.shape) for a in consts],
        out_specs=[row(w), col(w), row(w), row(w), row(LANES), col(2 * MLSTM_HEADS)],
        out_shape=[jax.ShapeDtypeStruct((t, w), BF16), jax.ShapeDtypeStruct((w, t), BF16),
                   jax.ShapeDtypeStruct((t, w), BF16), jax.ShapeDtypeStruct((t, w), BF16),
                   jax.ShapeDtypeStruct((t, LANES), F32),
                   jax.ShapeDtypeStruct((2 * MLSTM_HEADS, t), F32)],
        scratch_shapes=[pltpu.VMEM((L1_HALO + tm, w), F32)],
        compiler_params=_cparams(("arbitrary",), 48),
        name="l1_in",
    )(x, *consts)


def _mlstm_kernel(q_ref, kt_ref, v_ref, og_ref, lfc_ref, gr_ref, hg_ref, o_ref, c_scr, m_scr):
    ci = pl.program_id(1)
    L = q_ref.shape[0]
    d = MLSTM_HEAD_DIM
    nh = MLSTM_HEADS

    @pl.when(ci == 0)
    def _():
        c_scr[...] = jnp.zeros_like(c_scr)
        m_scr[...] = jnp.zeros_like(m_scr)

    r = lax.broadcasted_iota(I32, (L, L), 0)
    c = lax.broadcasted_iota(I32, (L, L), 1)
    tril = r >= c
    tri_lo = jnp.where(tril, 1.0, 0.0).astype(BF16)
    tri_up = jnp.where(r <= c, 1.0, 0.0).astype(BF16)

    bc_col = sum(_dot(tri_lo, p) for p in _split3(lfc_ref[...]))
    bc_row = sum(_dot(p, tri_up) for p in _split3(gr_ref[nh:2 * nh, :]))
    ones_blk = jnp.ones((L, d), BF16)

    for h in range(nh):
        sl = slice(h * d, (h + 1) * d)
        qh = q_ref[:, sl]
        kth = kt_ref[sl, :]
        v_aug = jnp.concatenate([v_ref[:, sl], ones_blk], axis=1)
        bcol = bc_col[:, h:h + 1]
        brow = bc_row[h:h + 1, :]
        srow = gr_ref[h:h + 1, :] - brow
        g = brow[:, L - 1:L]
        m_old = m_scr[h][:, 0:1]
        ct = c_scr[h]

        dmat = jnp.where(tril, bcol + srow, NEG)
        inter = bcol + m_old
        m_t = jnp.maximum(inter, jnp.max(dmat, axis=1, keepdims=True))
        w_inter = jnp.exp(inter - m_t)
        w_intra = jnp.exp(dmat - m_t) * _dot(qh, kth)
        res = w_inter * _dot(qh, ct.astype(BF16)) + _dot(w_intra.astype(BF16), v_aug)
        den = jnp.maximum(jnp.abs(res[:, d:]), jnp.exp(-m_t))
        hc = og_ref[:, sl].astype(F32) * (res[:, :d] / den)
        o_ref[:, sl] = _rms(hc, hg_ref[:, sl]).astype(o_ref.dtype)

        a_row = g + srow
        m_new = jnp.maximum(g + m_old, jnp.max(a_row, axis=1, keepdims=True))
        w_in = jnp.exp(a_row - m_new)
        kw = (kth.astype(F32) * w_in).astype(BF16)
        c_scr[h] = jnp.exp(g + m_old - m_new) * ct + _dot(kw, v_aug)
        m_scr[h] = jnp.broadcast_to(m_new, (1, LANES))


def _mlstm(q, kt, v, og, lfc, gr, hg, *, batch):
    t = q.shape[0]
    w = MLSTM_WIDTH
    nc = (t // batch) // CHUNK
    row = lambda n: pl.BlockSpec((CHUNK, n), lambda b, c: (b * nc + c, 0))
    col = lambda n: pl.BlockSpec((n, CHUNK), lambda b, c: (0, b * nc + c))
    return pl.pallas_call(
        _mlstm_kernel,
        grid=(batch, nc),
        in_specs=[row(w), col(w), row(w), row(w), row(LANES), col(2 * MLSTM_HEADS), _const_spec(hg.shape)],
        out_specs=row(w),
        out_shape=jax.ShapeDtypeStruct((t, w), BF16),
        scratch_shapes=[pltpu.VMEM((MLSTM_HEADS, MLSTM_HEAD_DIM, 2 * MLSTM_HEAD_DIM), F32),
                        pltpu.VMEM((MLSTM_HEADS, 1, LANES), F32)],
        compiler_params=_cparams(("arbitrary", "arbitrary"), 32),
        name="mlstm",
    )(q, kt, v, og, lfc, gr, hg)


def _l1_out_router_kernel(x_ref, hc_ref, wo_ref, g_ref, wr_ref, x3_ref, h3_ref, meta_ref, cnt_ref, carry_ref):
    i = pl.program_id(0)
    tm = x_ref.shape[0]

    @pl.when(i == 0)
    def _():
        carry_ref[...] = jnp.zeros_like(carry_ref)

    x3 = x_ref[...] + _dot(hc_ref[...], wo_ref[...])
    x3_ref[...] = x3
    h3 = _rms(x3, g_ref[...])
    h3_ref[...] = h3

    h_hi, h_mid, _ = _split3(h3)
    w_hi, w_mid, _ = _split3(wr_ref[...])
    logits = _dot(h_hi, w_hi) + (_dot(h_hi, w_mid) + _dot(h_mid, w_hi))

    lane = lax.broadcasted_iota(I32, (tm, LANES), 1)
    lane_f = lane.astype(F32)
    lg = jnp.where(lane < N_EXPERTS, logits, NEG)
    v1 = jnp.max(lg, axis=1, keepdims=True)
    i1 = jnp.min(jnp.where(lg == v1, lane_f, float(LANES)), axis=1, keepdims=True)
    lg2 = jnp.where(lane_f == i1, NEG, lg)
    v2 = jnp.max(lg2, axis=1, keepdims=True)
    i2 = jnp.min(jnp.where(lg2 == v2, lane_f, float(LANES)), axis=1, keepdims=True)
    e = jnp.exp(v2 - v1)
    g1 = 1.0 / (1.0 + e)
    g2 = e / (1.0 + e)

    oh1 = lane_f == i1
    oh2 = lane_f == i2
    oh = jnp.where(oh1 | oh2, 1.0, 0.0)
    r = lax.broadcasted_iota(I32, (tm, tm), 0)
    c = lax.broadcasted_iota(I32, (tm, tm), 1)
    strict = jnp.where(c < r, 1.0, 0.0).astype(BF16)
    pos = _dot(strict, oh.astype(BF16)) + carry_ref[0:1, :]
    rank1 = jnp.sum(jnp.where(oh1, pos, 0.0), axis=1, keepdims=True)
    rank2 = jnp.sum(jnp.where(oh2, pos, 0.0), axis=1, keepdims=True)
    total = carry_ref[0:1, :] + jnp.sum(oh, axis=0, keepdims=True)
    carry_ref[...] = jnp.broadcast_to(total, carry_ref.shape)
    cnt_ref[...] = jnp.broadcast_to(total, cnt_ref.shape)

    meta = jnp.zeros((tm, LANES), F32)
    for k, val in enumerate((i1, i2, g1, g2, rank1, rank2)):
        meta = jnp.where(lane == k, val, meta)
    meta_ref[...] = meta


def _l1_out_router(x, hc, wo, g, wr):
    t = x.shape[0]
    tm = TM_OUT1
    row = lambda n: pl.BlockSpec((tm, n), lambda i: (i, 0))
    return pl.pallas_call(
        _l1_out_router_kernel,
        grid=(t // tm,),
        in_specs=[row(D_MODEL), row(MLSTM_WIDTH), _const_spec(wo.shape), _const_spec(g.shape),
                  _const_spec(wr.shape)],
        out_specs=[row(D_MODEL), row(D_MODEL), row(LANES), pl.BlockSpec((SUBLANES, LANES), lambda i: (0, 0))],
        out_shape=[jax.ShapeDtypeStruct((t, D_MODEL), F32), jax.ShapeDtypeStruct((t, D_MODEL), F32),
                   jax.ShapeDtypeStruct((t, LANES), F32), jax.ShapeDtypeStruct((SUBLANES, LANES), F32)],
        scratch_shapes=[pltpu.VMEM((SUBLANES, LANES), F32)],
        compiler_params=_cparams(("arbitrary",), 32),
        name="l1_out_router",
    )(x, hc, wo, g, wr)


def _gather_row(h_hbm, x_scr, sem, slot, tok, r):
    return pltpu.make_async_copy(h_hbm.at[pl.ds(tok, 1), :], x_scr.at[slot, pl.ds(r, 1), :], sem.at[slot])


def _experts_kernel(te_ref, nused_ref, src0_ref, srcn_ref, h_hbm, wgu_ref, wd_ref, o_ref,
                    x_scr, xb_scr, gsem):
    i = pl.program_id(0)
    j = pl.program_id(1)
    nt = pl.num_programs(0)
    nf = pl.num_programs(1)
    tm = o_ref.shape[0]
    rows = x_scr.shape[1]
    valid = i < nused_ref[0]
    slot = i % 2

    def wait_tile(s):
        def wait(r, c):
            _gather_row(h_hbm, x_scr, gsem, s, 0, r).wait()
            return c
        lax.fori_loop(0, rows, wait, 0, unroll=8)

    def prefetch_next():
        for k in range(GATHER_ROWS_PER_STEP):
            r = j * GATHER_ROWS_PER_STEP + k
            _gather_row(h_hbm, x_scr, gsem, 1 - slot, srcn_ref[0, r], r).start()

    @pl.when((i == 0) & (j == 0))
    def _():
        def start(r, c):
            _gather_row(h_hbm, x_scr, gsem, 0, src0_ref[0, r], r).start()
            return c
        lax.fori_loop(0, rows, start, 0, unroll=8)

    @pl.when(j == 0)
    def _():
        wait_tile(slot)
        xb_scr[...] = x_scr[slot, 0:tm, :].astype(BF16)
        o_ref[...] = jnp.zeros_like(o_ref)

    @pl.when(valid)
    def _():
        gu = _dot(xb_scr[...], wgu_ref[...])
        prefetch_next()
        gate = gu[:, :TF_E]
        act = (gate * _sigmoid(gate) * gu[:, TF_E:]).astype(BF16)
        o_ref[...] += _dot(act, wd_ref[...])

    @pl.when(jnp.logical_not(valid))
    def _():
        prefetch_next()

    @pl.when((i == nt - 1) & (j == nf - 1))
    def _():
        wait_tile(1 - slot)


def _experts(tile_expert, nused, src, h3, wgu, wd):
    nt = src.shape[0] - 1
    nf = D_FF_EXPERT // TF_E
    rows = nf * GATHER_ROWS_PER_STEP

    def wcol(i, j, te, nu):
        return (te[i], jnp.where(i < nu[0], j, nf - 1), 0, 0)

    def wrow(i, j, te, nu):
        return (te[i], jnp.where(i < nu[0], j, nf - 1), 0)

    def smem_tile(index_map):
        return pl.BlockSpec((None, 1, src.shape[2]), index_map, memory_space=pltpu.SMEM)

    grid_spec = pltpu.PrefetchScalarGridSpec(
        num_scalar_prefetch=2,
        grid=(nt, nf),
        in_specs=[smem_tile(lambda i, j, te, nu: (0, 0, 0)), smem_tile(lambda i, j, te, nu: (i + 1, 0, 0)),
                  pl.BlockSpec(memory_space=pl.ANY),
                  pl.BlockSpec((None, None, D_MODEL, 2 * TF_E), wcol),
                  pl.BlockSpec((None, TF_E, D_MODEL), wrow)],
        out_specs=pl.BlockSpec((TM_E, D_MODEL), lambda i, j, te, nu: (i, 0)),
        scratch_shapes=[pltpu.VMEM((2, rows, D_MODEL), F32), pltpu.VMEM((TM_E, D_MODEL), BF16),
                        pltpu.SemaphoreType.DMA((2,))],
    )
    return pl.pallas_call(
        _experts_kernel,
        grid_spec=grid_spec,
        out_shape=jax.ShapeDtypeStruct((nt * TM_E, D_MODEL), F32),
        compiler_params=_cparams(("arbitrary", "arbitrary"), 48),
        name="experts",
    )(tile_expert, nused, src, src, h3, wgu, wd)


def _combine_kernel(pos0_ref, posn_ref, x_ref, meta_ref, g_ref, y_hbm, o_ref, y_scr, sem):
    i = pl.program_id(0)
    n = pl.num_programs(0)
    tm = x_ref.shape[0]
    slot = i % 2

    def row_copy(s, pos, r):
        return pltpu.make_async_copy(y_hbm.at[pl.ds(pos, 1), :], y_scr.at[s, pl.ds(r, 1), :], sem.at[s])

    def start_tile(pos_ref, s):
        def start(r, c):
            row_copy(s, pos_ref[0, r], r).start()
            return c
        lax.fori_loop(0, 2 * tm, start, 0, unroll=8)

    @pl.when(i == 0)
    def _():
        start_tile(pos0_ref, 0)

    @pl.when(i + 1 < n)
    def _():
        for r in range(2 * tm):
            row_copy(1 - slot, posn_ref[0, r], r).start()

    def wait(r, c):
        row_copy(slot, 0, r).wait()
        return c
    lax.fori_loop(0, 2 * tm, wait, 0, unroll=8)

    meta = meta_ref[...]
    y = x_ref[...] + meta[:, 2:3] * y_scr[slot, 0:tm, :] + meta[:, 3:4] * y_scr[slot, tm:2 * tm, :]
    o_ref[...] = _rms(y, g_ref[...])


def _combine(pos, x3, y, meta, g):
    t = x3.shape[0]
    tm = TM_FIN
    nt = t // tm
    row = lambda n: pl.BlockSpec((tm, n), lambda i: (i, 0))

    def smem_tile(index_map):
        return pl.BlockSpec((None, 1, 2 * tm), index_map, memory_space=pltpu.SMEM)

    return pl.pallas_call(
        _combine_kernel,
        grid=(nt,),
        in_specs=[smem_tile(lambda i: (0, 0, 0)), smem_tile(lambda i: (jnp.minimum(i + 1, nt - 1), 0, 0)),
                  row(D_MODEL), row(LANES), _const_spec(g.shape), pl.BlockSpec(memory_space=pl.ANY)],
        out_specs=row(D_MODEL),
        out_shape=jax.ShapeDtypeStruct((t, D_MODEL), F32),
        scratch_shapes=[pltpu.VMEM((2, 2 * tm, D_MODEL), F32), pltpu.SemaphoreType.DMA((2,))],
        compiler_params=_cparams(("arbitrary",), 40),
        name="combine_final_norm",
    )(pos, pos, x3, meta, g, y)


def _route(meta, cnt, t):
    idx = meta[:, 0:2].astype(I32)
    rank = meta[:, 4:6].astype(I32)
    counts = cnt[0, :N_EXPERTS].astype(I32)
    tiles_e = (counts + TM_E - 1) // TM_E
    tile_end = jnp.cumsum(tiles_e)
    row_off = (tile_end - tiles_e) * TM_E
    nt = 2 * t // TM_E + N_EXPERTS
    p = nt * TM_E
    pos = row_off[idx] + rank
    src = jnp.zeros((p,), I32).at[pos.reshape(-1)].set(jnp.repeat(jnp.arange(t, dtype=I32), 2),
                                                       unique_indices=True)
    src = jnp.pad(src.reshape(nt, 1, TM_E), ((0, 1), (0, 0), (0, EXPERT_SRC_WIDTH - TM_E)))
    tiles = jnp.arange(nt, dtype=I32)
    tile_expert = jnp.minimum(jnp.sum((tiles[:, None] >= tile_end[None, :]).astype(I32), axis=1),
                              N_EXPERTS - 1)
    nused = tile_end[-1:].astype(I32)
    pos_tiles = pos.reshape(t // TM_FIN, TM_FIN, 2).transpose(0, 2, 1).reshape(t // TM_FIN, 1, 2 * TM_FIN)
    return tile_expert, nused, src, pos_tiles


def kernel(x, l0_norm1_g, l0_w_in, l0_b_f, l0_dw_w, l0_dw_b, l0_cln_g, l0_cln_b, l0_w_out, l0_norm2_g, l0_w_gate, l0_w_up, l0_w_down, l1_norm1_g, l1_w_in, l1_b_i, l1_b_f, l1_conv_w, l1_conv_b, l1_wq_head, l1_wk_head, l1_hnorm_g, l1_w_out, l1_norm2_g, l1_router, l1_e_gate, l1_e_up, l1_e_down, final_norm_g):
    batch, seq, d = x.shape
    t = batch * seq
    xf = x.reshape(t, d)
    vec = lambda a: a.reshape(1, -1).astype(F32)
    bf = lambda a: a.astype(BF16)

    fw = FOX_WIDTH
    o_f = 3 * fw
    o_a = o_f + FOX_HEADS
    o_g = o_a + CONV_CHANNELS
    w_qv = jnp.concatenate([l0_w_in[:, :fw], l0_w_in[:, 2 * fw:o_f]], axis=1)
    q, kta, v, u = _l0_in(
        xf, vec(l0_norm1_g), bf(w_qv), bf(l0_w_in[:, fw:2 * fw].T), bf(l0_w_in[:, o_f:o_a].T),
        l0_b_f.reshape(-1, 1).astype(F32), bf(l0_w_in[:, o_a:o_g]), bf(l0_w_in[:, o_g:]), batch=batch)
    att = _fox_attention(q, kta, v, batch=batch)
    dw_w = jnp.concatenate([l0_dw_w, jnp.zeros((1, CONV_CHANNELS), F32)], axis=0)
    uc = _conv_module(u, dw_w, vec(l0_dw_b), vec(l0_cln_g), vec(l0_cln_b), batch=batch)
    x2 = _l0_out_ffn(xf, att, uc, bf(l0_w_out[:fw]), bf(l0_w_out[fw:]), vec(l0_norm2_g),
                     bf(l0_w_gate), bf(l0_w_up), bf(l0_w_down))

    w = MLSTM_WIDTH
    nh = MLSTM_HEADS
    w_i = l1_w_in[:, 3 * w:3 * w + nh]
    w_f = l1_w_in[:, 3 * w + nh:]
    wfc = bf(jnp.pad(w_f, ((0, 0), (0, LANES - nh))))
    bfc = jnp.pad(l1_b_f, (0, LANES - nh)).reshape(1, LANES).astype(F32)
    wgr = bf(jnp.concatenate([w_i.T, w_f.T], axis=0))
    bgr = jnp.concatenate([l1_b_i, l1_b_f]).reshape(-1, 1).astype(F32)
    ql, kt, vl, og, lfc, gr = _l1_in(
        x2, vec(l1_norm1_g), bf(l1_w_in[:, :3 * w]), wfc, bfc, wgr, bgr, l1_conv_w.astype(F32),
        vec(l1_conv_b), bf(l1_wq_head), bf(jnp.swapaxes(l1_wk_head, 1, 2)), batch=batch)
    hc = _mlstm(ql, kt, vl, og, lfc, gr, vec(l1_hnorm_g), batch=batch)
    wr = jnp.pad(l1_router.astype(F32), ((0, 0), (0, LANES - N_EXPERTS)))
    x3, h3, meta, cnt = _l1_out_router(x2, hc, bf(l1_w_out), vec(l1_norm2_g), wr)

    tile_expert, nused, src, pos_tiles = _route(meta, cnt, t)
    nf = D_FF_EXPERT // TF_E
    tiled = lambda a: a.reshape(N_EXPERTS, D_MODEL, nf, TF_E).transpose(0, 2, 1, 3)
    wgu = bf(jnp.concatenate([tiled(l1_e_gate), tiled(l1_e_up)], axis=-1))
    y = _experts(tile_expert, nused, src, h3, wgu, bf(l1_e_down))
    out = _combine(pos_tiles, x3, y, meta, vec(final_norm_g))
    return out.reshape(batch, seq, d)
```

```python
import functools

import jax
import jax.numpy as jnp
from jax import lax
from jax.experimental import pallas as pl
from jax.experimental.pallas import tpu as pltpu

F32 = jnp.float32
BF16 = jnp.bfloat16
I32 = jnp.int32

EPS = 1e-6
NEG = -1e30
LOG2E = 1.4426950408889634

D_MODEL = 1024
FOX_HEADS = 8
FOX_HEAD_DIM = 64
FOX_WIDTH = FOX_HEADS * FOX_HEAD_DIM
CONV_CHANNELS = 512
CONV_WIDTH = 31
MLSTM_HEADS = 8
MLSTM_HEAD_DIM = 128
MLSTM_WIDTH = MLSTM_HEADS * MLSTM_HEAD_DIM
MLSTM_CONV_WIDTH = 4
D_FF = 2816
N_EXPERTS = 8
D_FF_EXPERT = 3584

LANES = 128
SUBLANES = 8
MIB = 1024 * 1024

TM_IN0 = 512
TQ = 512
FOX_Q_HALVES = 2
FOX_KAUG = 256
TM_CONV = 512
CONV_ROWS = 64
CONV_HALO = 32
TM_FFN0 = 512
TM_IN1 = 512
L1_HALO = 8
CHUNK = 256
TM_OUT1 = 512
TM_E = 512
TF_E = 512
TM_FIN = 512


def _cparams(sem, vmem_mib):
    return pltpu.CompilerParams(dimension_semantics=sem, vmem_limit_bytes=vmem_mib * MIB)


def _rms(x, g):
    return x * lax.rsqrt(jnp.mean(x * x, axis=-1, keepdims=True) + EPS) * g


def _sigmoid(x):
    return 1.0 / (1.0 + jnp.exp(-x))


def _log_sigmoid(x):
    return jnp.minimum(x, 0.0) - jnp.log(1.0 + jnp.exp(-jnp.abs(x)))


def _dot(a, b):
    return jnp.dot(a, b, preferred_element_type=F32)


def _dot_nt(a, b):
    return lax.dot_general(a, b, (((1,), (1,)), ((), ())), preferred_element_type=F32)


def _split3(x):
    hi = x.astype(BF16)
    r1 = x - hi.astype(F32)
    mid = r1.astype(BF16)
    lo = (r1 - mid.astype(F32)).astype(BF16)
    return hi, mid, lo


def _const_spec(shape):
    nd = len(shape)
    return pl.BlockSpec(shape, lambda *_: (0,) * nd, pipeline_mode=pl.Buffered(1))


def _l0_in_kernel(x_ref, g_ref, wqv_ref, wkt_ref, wft_ref, bf_ref, wa_ref, wg_ref,
                  q_ref, kta_ref, v_ref, u_ref, carry_ref, *, tiles_per_batch):
    i = pl.program_id(0)
    tm = x_ref.shape[0]
    hb = _rms(x_ref[...], g_ref[...]).astype(BF16)
    qv = _dot(hb, wqv_ref[...])
    q_ref[...] = (qv[:, :FOX_WIDTH] * (FOX_HEAD_DIM ** -0.5 * LOG2E)).astype(BF16)
    v_ref[...] = qv[:, FOX_WIDTH:].astype(BF16)
    kt = _dot_nt(wkt_ref[...], hb).astype(BF16)
    u_ref[...] = _dot(hb, wa_ref[...]) * _sigmoid(_dot(hb, wg_ref[...]))

    logf = _log_sigmoid(_dot_nt(wft_ref[...], hb) + bf_ref[...])
    lane = lax.broadcasted_iota(I32, logf.shape, 1)
    cum = logf
    shift = 1
    while shift < tm:
        cum = cum + jnp.where(lane >= shift, pltpu.roll(cum, shift, 1), 0.0)
        shift *= 2

    @pl.when(i % tiles_per_batch == 0)
    def _():
        carry_ref[...] = jnp.zeros_like(carry_ref)

    c = cum + carry_ref[:, 0:1]
    carry_ref[...] = jnp.broadcast_to(c[:, tm - 1:tm], carry_ref.shape)

    pieces = [p.astype(F32) for p in _split3(c * LOG2E)]
    sub = lax.broadcasted_iota(I32, (2 * SUBLANES, tm), 0)
    zeros_tail = jnp.zeros((FOX_KAUG - LANES - 2 * SUBLANES, tm), BF16)
    for p in range(FOX_HEADS // 2):
        ext = jnp.zeros((2 * SUBLANES, tm), F32)
        for hh in range(2):
            for n, piece in enumerate(pieces):
                ext = jnp.where(sub == 3 * hh + n, piece[2 * p + hh:2 * p + hh + 1, :], ext)
        kta_ref[p, 0:LANES, :] = kt[p * LANES:(p + 1) * LANES, :]
        kta_ref[p, LANES:LANES + 2 * SUBLANES, :] = ext.astype(BF16)
        kta_ref[p, LANES + 2 * SUBLANES:, :] = zeros_tail


def _l0_in(x, g, wqv, wkt, wft, bf, wa, wg, *, batch):
    t = x.shape[0]
    tm = TM_IN0
    tpb = (t // batch) // tm
    pairs = FOX_HEADS // 2
    row = lambda w: pl.BlockSpec((tm, w), lambda i: (i, 0))
    consts = [g, wqv, wkt, wft, bf, wa, wg]
    return pl.pallas_call(
        functools.partial(_l0_in_kernel, tiles_per_batch=tpb),
        grid=(t // tm,),
        in_specs=[row(D_MODEL)] + [_const_spec(a.shape) for a in consts],
        out_specs=[row(FOX_WIDTH), pl.BlockSpec((pairs, FOX_KAUG, tm), lambda i: (0, 0, i)),
                   row(FOX_WIDTH), row(CONV_CHANNELS)],
        out_shape=[jax.ShapeDtypeStruct((t, FOX_WIDTH), BF16),
                   jax.ShapeDtypeStruct((pairs, FOX_KAUG, t), BF16),
                   jax.ShapeDtypeStruct((t, FOX_WIDTH), BF16),
                   jax.ShapeDtypeStruct((t, CONV_CHANNELS), F32)],
        scratch_shapes=[pltpu.VMEM((FOX_HEADS, LANES), F32)],
        compiler_params=_cparams(("arbitrary",), 40),
        name="l0_in",
    )(x, *consts)


def _fox_kernel(q_ref, kta_ref, v_ref, o_ref, q_scr, m_scr, acc_scr):
    qi = pl.program_id(2)
    tq = TQ
    tk = TQ
    lane = lax.broadcasted_iota(I32, (tq, LANES), 1)
    is_a = lane < FOX_HEAD_DIM
    lane_k = lax.broadcasted_iota(I32, (tk, LANES), 1)

    ext_a = jnp.where(lane < 3, -1.0, 0.0)
    ext_b = jnp.where((lane >= 3) & (lane < 6), -1.0, 0.0)
    for half in range(FOX_Q_HALVES):
        q2 = q_ref[half * tq:(half + 1) * tq, :].astype(F32)
        q_scr[2 * half] = jnp.concatenate([jnp.where(is_a, q2, 0.0), ext_a], axis=1).astype(BF16)
        q_scr[2 * half + 1] = jnp.concatenate([jnp.where(is_a, 0.0, q2), ext_b], axis=1).astype(BF16)
    m_scr[...] = jnp.full(m_scr.shape, NEG, F32)
    acc_scr[...] = jnp.zeros_like(acc_scr)

    def tile(ki, plan):
        start = pl.multiple_of(ki * tk, tk)
        kt = kta_ref[:, pl.ds(start, tk)]
        vf = v_ref[pl.ds(start, tk), :].astype(F32)
        v_augs = (jnp.where(lane_k < FOX_HEAD_DIM, vf, 1.0).astype(BF16),
                  jnp.where(lane_k < FOX_HEAD_DIM, 1.0, vf).astype(BF16))
        causal = (lax.broadcasted_iota(I32, (tq, tk), 1) <= lax.broadcasted_iota(I32, (tq, tk), 0))
        for half, masked in plan:
            for h in range(2):
                n = 2 * half + h
                s = _dot(q_scr[n], kt)
                if masked:
                    s = jnp.where(causal, s, NEG)
                m_old = m_scr[n]
                m_new = jnp.maximum(m_old, jnp.max(s, axis=1, keepdims=True))
                p = jnp.concatenate(
                    [jnp.exp2(s[:, j * LANES:(j + 1) * LANES] - m_new).astype(BF16) for j in range(tk // LANES)],
                    axis=1)
                acc_scr[n] = acc_scr[n] * jnp.exp2(m_old - m_new) + _dot(p, v_augs[h])
                m_scr[n] = m_new

    def body(ki, carry):
        tile(ki, tuple((half, False) for half in range(FOX_Q_HALVES)))
        return carry

    first_diag = FOX_Q_HALVES * qi
    lax.fori_loop(0, first_diag, body, 0)
    for d in range(FOX_Q_HALVES):
        tile(first_diag + d, ((d, True),) + tuple((half, False) for half in range(d + 1, FOX_Q_HALVES)))

    for half in range(FOX_Q_HALVES):
        acc_a = acc_scr[2 * half]
        acc_b = acc_scr[2 * half + 1]
        o = jnp.where(is_a, acc_a / pltpu.roll(acc_a, FOX_HEAD_DIM, 1),
                      acc_b / pltpu.roll(acc_b, FOX_HEAD_DIM, 1))
        o_ref[half * tq:(half + 1) * tq, :] = o.astype(o_ref.dtype)


def _fox_attention(q, kta, v, *, batch):
    t = q.shape[0]
    seq = t // batch
    rows = FOX_Q_HALVES * TQ
    nq = seq // rows
    pairs = FOX_HEADS // 2
    nstate = 2 * FOX_Q_HALVES
    return pl.pallas_call(
        _fox_kernel,
        grid=(batch, pairs, nq),
        in_specs=[pl.BlockSpec((rows, LANES), lambda b, p, i: (b * nq + i, p)),
                  pl.BlockSpec((None, FOX_KAUG, seq), lambda b, p, i: (p, 0, b)),
                  pl.BlockSpec((seq, LANES), lambda b, p, i: (b, p))],
        out_specs=pl.BlockSpec((rows, LANES), lambda b, p, i: (b * nq + i, p)),
        out_shape=jax.ShapeDtypeStruct((t, FOX_WIDTH), BF16),
        scratch_shapes=[pltpu.VMEM((nstate, TQ, FOX_KAUG), BF16),
                        pltpu.VMEM((nstate, TQ, LANES), F32), pltpu.VMEM((nstate, TQ, LANES), F32)],
        compiler_params=_cparams(("arbitrary", "arbitrary", "arbitrary"), 40),
        name="fox_attention",
    )(q, kta, v)


def _conv_mod_kernel(u_ref, w_ref, b_ref, g_ref, beta_ref, o_ref, win_ref, *, tiles_per_batch):
    i = pl.program_id(0)
    tm = u_ref.shape[0]

    @pl.when(i % tiles_per_batch == 0)
    def _():
        win_ref[0:CONV_HALO, :] = jnp.zeros((CONV_HALO, CONV_CHANNELS), F32)

    win_ref[CONV_HALO:CONV_HALO + tm, :] = u_ref[...]
    first = CONV_HALO - (CONV_WIDTH - 1)
    for c in range(tm // CONV_ROWS):
        r0 = c * CONV_ROWS
        acc = jnp.broadcast_to(b_ref[...], (CONV_ROWS, CONV_CHANNELS))
        for j in range(CONV_WIDTH):
            acc = acc + win_ref[r0 + first + j:r0 + first + j + CONV_ROWS, :] * w_ref[j:j + 1, :]
        mu = jnp.mean(acc, axis=-1, keepdims=True)
        d = acc - mu
        var = jnp.mean(d * d, axis=-1, keepdims=True)
        y = d * lax.rsqrt(var + EPS) * g_ref[...] + beta_ref[...]
        o_ref[r0:r0 + CONV_ROWS, :] = (y * _sigmoid(y)).astype(o_ref.dtype)
    win_ref[0:CONV_HALO, :] = win_ref[tm:tm + CONV_HALO, :]


def _conv_module(u, w, b, g, beta, *, batch):
    t = u.shape[0]
    tm = TM_CONV
    tpb = (t // batch) // tm
    return pl.pallas_call(
        functools.partial(_conv_mod_kernel, tiles_per_batch=tpb),
        grid=(t // tm,),
        in_specs=[pl.BlockSpec((tm, CONV_CHANNELS), lambda i: (i, 0)), _const_spec(w.shape),
                  _const_spec(b.shape), _const_spec(g.shape), _const_spec(beta.shape)],
        out_specs=pl.BlockSpec((tm, CONV_CHANNELS), lambda i: (i, 0)),
        out_shape=jax.ShapeDtypeStruct((t, CONV_CHANNELS), BF16),
        scratch_shapes=[pltpu.VMEM((CONV_HALO + tm, CONV_CHANNELS), F32)],
        compiler_params=_cparams(("arbitrary",), 32),
        name="conv_module",
    )(u, w, b, g, beta)


def _l0_out_ffn_kernel(x_ref, att_ref, u_ref, woa_ref, wou_ref, g_ref, wg_ref, wu_ref, wd_ref, o_ref):
    x1 = x_ref[...] + _dot(att_ref[...], woa_ref[...]) + _dot(u_ref[...], wou_ref[...])
    hb = _rms(x1, g_ref[...]).astype(BF16)
    gate = _dot(hb, wg_ref[...])
    act = (gate * _sigmoid(gate) * _dot(hb, wu_ref[...])).astype(BF16)
    o_ref[...] = x1 + _dot(act, wd_ref[...])


def _l0_out_ffn(x, att, u, woa, wou, g, wg, wu, wd):
    t = x.shape[0]
    tm = TM_FFN0
    row = lambda w: pl.BlockSpec((tm, w), lambda i: (i, 0))
    return pl.pallas_call(
        _l0_out_ffn_kernel,
        grid=(t // tm,),
        in_specs=[row(D_MODEL), row(FOX_WIDTH), row(CONV_CHANNELS), _const_spec(woa.shape),
                  _const_spec(wou.shape), _const_spec(g.shape), _const_spec(wg.shape),
                  _const_spec(wu.shape), _const_spec(wd.shape)],
        out_specs=row(D_MODEL),
        out_shape=jax.ShapeDtypeStruct((t, D_MODEL), F32),
        compiler_params=_cparams(("arbitrary",), 56),
        name="l0_out_ffn",
    )(x, att, u, woa, wou, g, wg, wu, wd)


def _l1_in_kernel(x_ref, g_ref, wuvo_ref, wfc_ref, bfc_ref, wgr_ref, bgr_ref, cw_ref, cb_ref,
                  wq_ref, wkt_ref, q_ref, kt_ref, v_ref, og_ref, lfc_ref, gr_ref, win_ref,
                  *, tiles_per_batch):
    i = pl.program_id(0)
    tm = x_ref.shape[0]
    w = MLSTM_WIDTH
    hb = _rms(x_ref[...], g_ref[...]).astype(BF16)
    uvo = _dot(hb, wuvo_ref[...])
    v_ref[...] = uvo[:, w:2 * w].astype(BF16)
    og_ref[...] = _sigmoid(uvo[:, 2 * w:]).astype(BF16)

    lfc_ref[...] = _log_sigmoid(_dot(hb, wfc_ref[...]) + bfc_ref[...])
    grow = _dot_nt(wgr_ref[...], hb) + bgr_ref[...]
    is_i = lax.broadcasted_iota(I32, grow.shape, 0) < MLSTM_HEADS
    gr_ref[...] = jnp.where(is_i, grow, _log_sigmoid(grow))

    @pl.when(i % tiles_per_batch == 0)
    def _():
        win_ref[0:L1_HALO, :] = jnp.zeros((L1_HALO, w), F32)

    win_ref[L1_HALO:L1_HALO + tm, :] = uvo[:, :w]
    first = L1_HALO - (MLSTM_CONV_WIDTH - 1)
    acc = jnp.broadcast_to(cb_ref[...], (tm, w))
    for j in range(MLSTM_CONV_WIDTH):
        acc = acc + win_ref[first + j:first + j + tm, :] * cw_ref[j:j + 1, :]
    win_ref[0:L1_HALO, :] = win_ref[tm:tm + L1_HALO, :]
    uc = (acc * _sigmoid(acc)).astype(BF16)
    d = MLSTM_HEAD_DIM
    for h in range(MLSTM_HEADS):
        uh = uc[:, h * d:(h + 1) * d]
        q_ref[:, h * d:(h + 1) * d] = _dot(uh, wq_ref[h]).astype(BF16)
        kt_ref[h * d:(h + 1) * d, :] = (_dot_nt(wkt_ref[h], uh) * (d ** -0.5)).astype(BF16)


def _l1_in(x, g, wuvo, wfc, bfc, wgr, bgr, cw, cb, wq, wkt, *, batch):
    t = x.shape[0]
    tm = TM_IN1
    tpb = (t // batch) // tm
    w = MLSTM_WIDTH
    row = lambda n: pl.BlockSpec((tm, n), lambda i: (i, 0))
    col = lambda n: pl.BlockSpec((n, tm), lambda i: (0, i))
    consts = [g, wuvo, wfc, bfc, wgr, bgr, cw, cb, wq, wkt]
    return pl.pallas_call(
        functools.partial(_l1_in_kernel, tiles_per_batch=tpb),
        grid=(t // tm,),
        in_specs=[row(D_MODEL)] + [_const_spec(a.shape) for a in consts],
        out_specs=[row(w), col(w), row(w), row(w), row(LANES), col(2 * MLSTM_HEADS)],
        out_shape=[jax.ShapeDtypeStruct((t, w), BF16), jax.ShapeDtypeStruct((w, t), BF16),
                   jax.ShapeDtypeStruct((t, w), BF16), jax.ShapeDtypeStruct((t, w), BF16),
                   jax.ShapeDtypeStruct((t, LANES), F32),
                   jax.ShapeDtypeStruct((2 * MLSTM_HEADS, t), F32)],
        scratch_shapes=[pltpu.VMEM((L1_HALO + tm, w), F32)],
        compiler_params=_cparams(("arbitrary",), 48),
        name="l1_in",
    )(x, *consts)


def _mlstm_kernel(q_ref, kt_ref, v_ref, og_ref, lfc_ref, gr_ref, hg_ref, o_ref, c_scr, m_scr):
    ci = pl.program_id(1)
    L = q_ref.shape[0]
    d = MLSTM_HEAD_DIM
    nh = MLSTM_HEADS

    @pl.when(ci == 0)
    def _():
        c_scr[...] = jnp.zeros_like(c_scr)
        m_scr[...] = jnp.zeros_like(m_scr)

    r = lax.broadcasted_iota(I32, (L, L), 0)
    c = lax.broadcasted_iota(I32, (L, L), 1)
    tril = r >= c
    tri_lo = jnp.where(tril, 1.0, 0.0).astype(BF16)
    tri_up = jnp.where(r <= c, 1.0, 0.0).astype(BF16)

    bc_col = sum(_dot(tri_lo, p) for p in _split3(lfc_ref[...]))
    bc_row = sum(_dot(p, tri_up) for p in _split3(gr_ref[nh:2 * nh, :]))
    ones_blk = jnp.ones((L, d), BF16)

    for h in range(nh):
        sl = slice(h * d, (h + 1) * d)
        qh = q_ref[:, sl]
        kth = kt_ref[sl, :]
        v_aug = jnp.concatenate([v_ref[:, sl], ones_blk], axis=1)
        bcol = bc_col[:, h:h + 1]
        brow = bc_row[h:h + 1, :]
        srow = gr_ref[h:h + 1, :] - brow
        g = brow[:, L - 1:L]
        m_old = m_scr[h][:, 0:1]
        ct = c_scr[h]

        dmat = jnp.where(tril, bcol + srow, NEG)
        inter = bcol + m_old
        m_t = jnp.maximum(inter, jnp.max(dmat, axis=1, keepdims=True))
        w_inter = jnp.exp(inter - m_t)
        w_intra = jnp.exp(dmat - m_t) * _dot(qh, kth)
        res = w_inter * _dot(qh, ct.astype(BF16)) + _dot(w_intra.astype(BF16), v_aug)
        den = jnp.maximum(jnp.abs(res[:, d:]), jnp.exp(-m_t))
        hc = og_ref[:, sl].astype(F32) * (res[:, :d] / den)
        o_ref[:, sl] = _rms(hc, hg_ref[:, sl]).astype(o_ref.dtype)

        a_row = g + srow
        m_new = jnp.maximum(g + m_old, jnp.max(a_row, axis=1, keepdims=True))
        w_in = jnp.exp(a_row - m_new)
        kw = (kth.astype(F32) * w_in).astype(BF16)
        c_scr[h] = jnp.exp(g + m_old - m_new) * ct + _dot(kw, v_aug)
        m_scr[h] = jnp.broadcast_to(m_new, (1, LANES))


def _mlstm(q, kt, v, og, lfc, gr, hg, *, batch):
    t = q.shape[0]
    w = MLSTM_WIDTH
    nc = (t // batch) // CHUNK
    row = lambda n: pl.BlockSpec((CHUNK, n), lambda b, c: (b * nc + c, 0))
    col = lambda n: pl.BlockSpec((n, CHUNK), lambda b, c: (0, b * nc + c))
    return pl.pallas_call(
        _mlstm_kernel,
        grid=(batch, nc),
        in_specs=[row(w), col(w), row(w), row(w), row(LANES), col(2 * MLSTM_HEADS), _const_spec(hg.shape)],
        out_specs=row(w),
        out_shape=jax.ShapeDtypeStruct((t, w), BF16),
        scratch_shapes=[pltpu.VMEM((MLSTM_HEADS, MLSTM_HEAD_DIM, 2 * MLSTM_HEAD_DIM), F32),
                        pltpu.VMEM((MLSTM_HEADS, 1, LANES), F32)],
        compiler_params=_cparams(("arbitrary", "arbitrary"), 32),
        name="mlstm",
    )(q, kt, v, og, lfc, gr, hg)


def _l1_out_router_kernel(x_ref, hc_ref, wo_ref, g_ref, wr_ref, x3_ref, h3_ref, meta_ref, cnt_ref, carry_ref):
    i = pl.program_id(0)
    tm = x_ref.shape[0]

    @pl.when(i == 0)
    def _():
        carry_ref[...] = jnp.zeros_like(carry_ref)

    x3 = x_ref[...] + _dot(hc_ref[...], wo_ref[...])
    x3_ref[...] = x3
    h3 = _rms(x3, g_ref[...])
    h3_ref[...] = h3

    h_hi, h_mid, _ = _split3(h3)
    w_hi, w_mid, _ = _split3(wr_ref[...])
    logits = _dot(h_hi, w_hi) + (_dot(h_hi, w_mid) + _dot(h_mid, w_hi))

    lane = lax.broadcasted_iota(I32, (tm, LANES), 1)
    lane_f = lane.astype(F32)
    lg = jnp.where(lane < N_EXPERTS, logits, NEG)
    v1 = jnp.max(lg, axis=1, keepdims=True)
    i1 = jnp.min(jnp.where(lg == v1, lane_f, float(LANES)), axis=1, keepdims=True)
    lg2 = jnp.where(lane_f == i1, NEG, lg)
    v2 = jnp.max(lg2, axis=1, keepdims=True)
    i2 = jnp.min(jnp.where(lg2 == v2, lane_f, float(LANES)), axis=1, keepdims=True)
    e = jnp.exp(v2 - v1)
    g1 = 1.0 / (1.0 + e)
    g2 = e / (1.0 + e)

    oh1 = lane_f == i1
    oh2 = lane_f == i2
    oh = jnp.where(oh1 | oh2, 1.0, 0.0)
    r = lax.broadcasted_iota(I32, (tm, tm), 0)
    c = lax.broadcasted_iota(I32, (tm, tm), 1)
    strict = jnp.where(c < r, 1.0, 0.0).astype(BF16)
    pos = _dot(strict, oh.astype(BF16)) + carry_ref[0:1, :]
    rank1 = jnp.sum(jnp.where(oh1, pos, 0.0), axis=1, keepdims=True)
    rank2 = jnp.sum(jnp.where(oh2, pos, 0.0), axis=1, keepdims=True)
    total = carry_ref[0:1, :] + jnp.sum(oh, axis=0, keepdims=True)
    carry_ref[...] = jnp.broadcast_to(total, carry_ref.shape)
    cnt_ref[...] = jnp.broadcast_to(total, cnt_ref.shape)

    meta = jnp.zeros((tm, LANES), F32)
    for k, val in enumerate((i1, i2, g1, g2, rank1, rank2)):
        meta = jnp.where(lane == k, val, meta)
    meta_ref[...] = meta


def _l1_out_router(x, hc, wo, g, wr):
    t = x.shape[0]
    tm = TM_OUT1
    row = lambda n: pl.BlockSpec((tm, n), lambda i: (i, 0))
    return pl.pallas_call(
        _l1_out_router_kernel,
        grid=(t // tm,),
        in_specs=[row(D_MODEL), row(MLSTM_WIDTH), _const_spec(wo.shape), _const_spec(g.shape),
                  _const_spec(wr.shape)],
        out_specs=[row(D_MODEL), row(D_MODEL), row(LANES), pl.BlockSpec((SUBLANES, LANES), lambda i: (0, 0))],
        out_shape=[jax.ShapeDtypeStruct((t, D_MODEL), F32), jax.ShapeDtypeStruct((t, D_MODEL), F32),
                   jax.ShapeDtypeStruct((t, LANES), F32), jax.ShapeDtypeStruct((SUBLANES, LANES), F32)],
        scratch_shapes=[pltpu.VMEM((SUBLANES, LANES), F32)],
        compiler_params=_cparams(("arbitrary",), 32),
        name="l1_out_router",
    )(x, hc, wo, g, wr)


def _gather_rows(tab0_ref, tabn_ref, src_hbm, buf, sem):
    i = pl.program_id(0)
    n = pl.num_programs(0)
    rows = buf.shape[1]
    slot = i % 2

    def row_copy(s, src_row, r):
        return pltpu.make_async_copy(src_hbm.at[pl.ds(src_row, 1), :], buf.at[s, pl.ds(r, 1), :], sem.at[s])

    @pl.when(i == 0)
    def _():
        def start(r, c):
            row_copy(0, tab0_ref[0, r], r).start()
            return c
        lax.fori_loop(0, rows, start, 0, unroll=8)

    @pl.when(i + 1 < n)
    def _():
        for r in range(rows):
            row_copy(1 - slot, tabn_ref[0, r], r).start()

    def wait(r, c):
        row_copy(slot, 0, r).wait()
        return c
    lax.fori_loop(0, rows, wait, 0, unroll=8)
    return slot


def _gather_specs(table, n):
    width = table.shape[2]

    def smem_tile(index_map):
        return pl.BlockSpec((None, 1, width), index_map, memory_space=pltpu.SMEM)

    return [smem_tile(lambda i: (0, 0, 0)), smem_tile(lambda i: (jnp.minimum(i + 1, n - 1), 0, 0))]


def _dispatch_kernel(tab0_ref, tabn_ref, h_hbm, o_ref, buf, sem):
    slot = _gather_rows(tab0_ref, tabn_ref, h_hbm, buf, sem)
    o_ref[...] = buf[slot].astype(o_ref.dtype)


def _dispatch(src, h3):
    nt = src.shape[0]
    return pl.pallas_call(
        _dispatch_kernel,
        grid=(nt,),
        in_specs=_gather_specs(src, nt) + [pl.BlockSpec(memory_space=pl.ANY)],
        out_specs=pl.BlockSpec((TM_E, D_MODEL), lambda i: (i, 0)),
        out_shape=jax.ShapeDtypeStruct((nt * TM_E, D_MODEL), BF16),
        scratch_shapes=[pltpu.VMEM((2, TM_E, D_MODEL), F32), pltpu.SemaphoreType.DMA((2,))],
        compiler_params=_cparams(("arbitrary",), 32),
        name="dispatch",
    )(src, src, h3)


def _experts_kernel(te_ref, nused_ref, x_ref, wg_ref, wu_ref, wd_ref, o_ref):
    i = pl.program_id(0)
    j = pl.program_id(1)

    @pl.when(j == 0)
    def _():
        o_ref[...] = jnp.zeros_like(o_ref)

    @pl.when(i < nused_ref[0])
    def _():
        xb = x_ref[...]
        gate = _dot(xb, wg_ref[...])
        act = (gate * _sigmoid(gate) * _dot(xb, wu_ref[...])).astype(BF16)
        o_ref[...] += _dot(act, wd_ref[...])


def _experts(tile_expert, nused, xs, wg, wu, wd):
    nt = xs.shape[0] // TM_E
    nf = D_FF_EXPERT // TF_E

    def wcol(i, j, te, nu):
        return (te[i], 0, jnp.where(i < nu[0], j, nf - 1))

    def wrow(i, j, te, nu):
        return (te[i], jnp.where(i < nu[0], j, nf - 1), 0)

    grid_spec = pltpu.PrefetchScalarGridSpec(
        num_scalar_prefetch=2,
        grid=(nt, nf),
        in_specs=[pl.BlockSpec((TM_E, D_MODEL), lambda i, j, te, nu: (i, 0)),
                  pl.BlockSpec((None, D_MODEL, TF_E), wcol), pl.BlockSpec((None, D_MODEL, TF_E), wcol),
                  pl.BlockSpec((None, TF_E, D_MODEL), wrow)],
        out_specs=pl.BlockSpec((TM_E, D_MODEL), lambda i, j, te, nu: (i, 0)),
    )
    return pl.pallas_call(
        _experts_kernel,
        grid_spec=grid_spec,
        out_shape=jax.ShapeDtypeStruct((nt * TM_E, D_MODEL), F32),
        compiler_params=_cparams(("arbitrary", "arbitrary"), 48),
        name="experts",
    )(tile_expert, nused, xs, wg, wu, wd)


def _combine_kernel(pos0_ref, posn_ref, x_ref, meta_ref, g_ref, y_hbm, o_ref, y_scr, sem):
    tm = x_ref.shape[0]
    slot = _gather_rows(pos0_ref, posn_ref, y_hbm, y_scr, sem)
    meta = meta_ref[...]
    y = x_ref[...] + meta[:, 2:3] * y_scr[slot, 0:tm, :] + meta[:, 3:4] * y_scr[slot, tm:2 * tm, :]
    o_ref[...] = _rms(y, g_ref[...])


def _combine(pos, x3, y, meta, g):
    t = x3.shape[0]
    tm = TM_FIN
    nt = t // tm
    row = lambda n: pl.BlockSpec((tm, n), lambda i: (i, 0))
    return pl.pallas_call(
        _combine_kernel,
        grid=(nt,),
        in_specs=_gather_specs(pos, nt) + [row(D_MODEL), row(LANES), _const_spec(g.shape),
                                           pl.BlockSpec(memory_space=pl.ANY)],
        out_specs=row(D_MODEL),
        out_shape=jax.ShapeDtypeStruct((t, D_MODEL), F32),
        scratch_shapes=[pltpu.VMEM((2, 2 * tm, D_MODEL), F32), pltpu.SemaphoreType.DMA((2,))],
        compiler_params=_cparams(("arbitrary",), 40),
        name="combine_final_norm",
    )(pos, pos, x3, meta, g, y)


def _route(meta, cnt, t):
    idx = meta[:, 0:2].astype(I32)
    rank = meta[:, 4:6].astype(I32)
    counts = cnt[0, :N_EXPERTS].astype(I32)
    tiles_e = (counts + TM_E - 1) // TM_E
    tile_end = jnp.cumsum(tiles_e)
    row_off = (tile_end - tiles_e) * TM_E
    nt = 2 * t // TM_E + N_EXPERTS
    p = nt * TM_E
    pos = row_off[idx] + rank
    src = jnp.zeros((p,), I32).at[pos.reshape(-1)].set(jnp.repeat(jnp.arange(t, dtype=I32), 2),
                                                       unique_indices=True)
    src = src.reshape(nt, 1, TM_E)
    tiles = jnp.arange(nt, dtype=I32)
    tile_expert = jnp.minimum(jnp.sum((tiles[:, None] >= tile_end[None, :]).astype(I32), axis=1),
                              N_EXPERTS - 1)
    nused = tile_end[-1:].astype(I32)
    pos_tiles = pos.reshape(t // TM_FIN, TM_FIN, 2).transpose(0, 2, 1).reshape(t // TM_FIN, 1, 2 * TM_FIN)
    return tile_expert, nused, src, pos_tiles


def kernel(x, l0_norm1_g, l0_w_in, l0_b_f, l0_dw_w, l0_dw_b, l0_cln_g, l0_cln_b, l0_w_out, l0_norm2_g, l0_w_gate, l0_w_up, l0_w_down, l1_norm1_g, l1_w_in, l1_b_i, l1_b_f, l1_conv_w, l1_conv_b, l1_wq_head, l1_wk_head, l1_hnorm_g, l1_w_out, l1_norm2_g, l1_router, l1_e_gate, l1_e_up, l1_e_down, final_norm_g):
    batch, seq, d = x.shape
    t = batch * seq
    xf = x.reshape(t, d)
    vec = lambda a: a.reshape(1, -1).astype(F32)
    bf = lambda a: a.astype(BF16)

    fw = FOX_WIDTH
    o_f = 3 * fw
    o_a = o_f + FOX_HEADS
    o_g = o_a + CONV_CHANNELS
    w_qv = jnp.concatenate([l0_w_in[:, :fw], l0_w_in[:, 2 * fw:o_f]], axis=1)
    q, kta, v, u = _l0_in(
        xf, vec(l0_norm1_g), bf(w_qv), bf(l0_w_in[:, fw:2 * fw].T), bf(l0_w_in[:, o_f:o_a].T),
        l0_b_f.reshape(-1, 1).astype(F32), bf(l0_w_in[:, o_a:o_g]), bf(l0_w_in[:, o_g:]), batch=batch)
    att = _fox_attention(q, kta, v, batch=batch)
    dw_w = jnp.concatenate([l0_dw_w, jnp.zeros((1, CONV_CHANNELS), F32)], axis=0)
    uc = _conv_module(u, dw_w, vec(l0_dw_b), vec(l0_cln_g), vec(l0_cln_b), batch=batch)
    x2 = _l0_out_ffn(xf, att, uc, bf(l0_w_out[:fw]), bf(l0_w_out[fw:]), vec(l0_norm2_g),
                     bf(l0_w_gate), bf(l0_w_up), bf(l0_w_down))

    w = MLSTM_WIDTH
    nh = MLSTM_HEADS
    w_i = l1_w_in[:, 3 * w:3 * w + nh]
    w_f = l1_w_in[:, 3 * w + nh:]
    wfc = bf(jnp.pad(w_f, ((0, 0), (0, LANES - nh))))
    bfc = jnp.pad(l1_b_f, (0, LANES - nh)).reshape(1, LANES).astype(F32)
    wgr = bf(jnp.concatenate([w_i.T, w_f.T], axis=0))
    bgr = jnp.concatenate([l1_b_i, l1_b_f]).reshape(-1, 1).astype(F32)
    ql, kt, vl, og, lfc, gr = _l1_in(
        x2, vec(l1_norm1_g), bf(l1_w_in[:, :3 * w]), wfc, bfc, wgr, bgr, l1_conv_w.astype(F32),
        vec(l1_conv_b), bf(l1_wq_head), bf(jnp.swapaxes(l1_wk_head, 1, 2)), batch=batch)
    hc = _mlstm(ql, kt, vl, og, lfc, gr, vec(l1_hnorm_g), batch=batch)
    wr = jnp.pad(l1_router.astype(F32), ((0, 0), (0, LANES - N_EXPERTS)))
    x3, h3, meta, cnt = _l1_out_router(x2, hc, bf(l1_w_out), vec(l1_norm2_g), wr)

    tile_expert, nused, src, pos_tiles = _route(meta, cnt, t)
    xs = _dispatch(src, h3)
    y = _experts(tile_expert, nused, xs, bf(l1_e_gate), bf(l1_e_up), bf(l1_e_down))
    out = _combine(pos_tiles, x3, y, meta, vec(final_norm_g))
    return out.reshape(batch, seq, d)
```

```python
import functools

import jax
import jax.numpy as jnp
from jax import lax
from jax.experimental import pallas as pl
from jax.experimental.pallas import tpu as pltpu

F32 = jnp.float32
BF16 = jnp.bfloat16
I32 = jnp.int32

EPS = 1e-6
NEG = -1e30
LOG2E = 1.4426950408889634

D_MODEL = 1024
FOX_HEADS = 8
FOX_HEAD_DIM = 64
FOX_WIDTH = FOX_HEADS * FOX_HEAD_DIM
CONV_CHANNELS = 512
CONV_WIDTH = 31
MLSTM_HEADS = 8
MLSTM_HEAD_DIM = 128
MLSTM_WIDTH = MLSTM_HEADS * MLSTM_HEAD_DIM
MLSTM_CONV_WIDTH = 4
D_FF = 2816
N_EXPERTS = 8
D_FF_EXPERT = 3584

LANES = 128
SUBLANES = 8
MIB = 1024 * 1024

TM_IN0 = 512
TQ = 512
FOX_Q_HALVES = 2
FOX_KAUG = 256
TM_CONV = 512
CONV_ROWS = 64
CONV_HALO = 32
TM_FFN0 = 512
TM_IN1 = 512
L1_HALO = 8
CHUNK = 256
TM_OUT1 = 512
TM_E = 512
TF_E = 512
TM_DISPATCH = 512
TM_FIN = 512


def _cparams(sem, vmem_mib):
    return pltpu.CompilerParams(dimension_semantics=sem, vmem_limit_bytes=vmem_mib * MIB)


def _rms(x, g):
    return x * lax.rsqrt(jnp.mean(x * x, axis=-1, keepdims=True) + EPS) * g


def _sigmoid(x):
    return 1.0 / (1.0 + jnp.exp(-x))


def _log_sigmoid(x):
    return jnp.minimum(x, 0.0) - jnp.log(1.0 + jnp.exp(-jnp.abs(x)))


def _dot(a, b):
    return jnp.dot(a, b, preferred_element_type=F32)


def _dot_nt(a, b):
    return lax.dot_general(a, b, (((1,), (1,)), ((), ())), preferred_element_type=F32)


def _split3(x):
    hi = x.astype(BF16)
    r1 = x - hi.astype(F32)
    mid = r1.astype(BF16)
    lo = (r1 - mid.astype(F32)).astype(BF16)
    return hi, mid, lo


def _const_spec(shape):
    nd = len(shape)
    return pl.BlockSpec(shape, lambda *_: (0,) * nd, pipeline_mode=pl.Buffered(1))


def _l0_in_kernel(x_ref, g_ref, wqv_ref, wkt_ref, wft_ref, bf_ref, wa_ref, wg_ref,
                  q_ref, kta_ref, v_ref, u_ref, carry_ref, *, tiles_per_batch):
    i = pl.program_id(0)
    tm = x_ref.shape[0]
    hb = _rms(x_ref[...], g_ref[...]).astype(BF16)
    qv = _dot(hb, wqv_ref[...])
    q_ref[...] = (qv[:, :FOX_WIDTH] * (FOX_HEAD_DIM ** -0.5 * LOG2E)).astype(BF16)
    v_ref[...] = qv[:, FOX_WIDTH:].astype(BF16)
    kt = _dot_nt(wkt_ref[...], hb).astype(BF16)
    u_ref[...] = _dot(hb, wa_ref[...]) * _sigmoid(_dot(hb, wg_ref[...]))

    logf = _log_sigmoid(_dot_nt(wft_ref[...], hb) + bf_ref[...])
    lane = lax.broadcasted_iota(I32, logf.shape, 1)
    cum = logf
    shift = 1
    while shift < tm:
        cum = cum + jnp.where(lane >= shift, pltpu.roll(cum, shift, 1), 0.0)
        shift *= 2

    @pl.when(i % tiles_per_batch == 0)
    def _():
        carry_ref[...] = jnp.zeros_like(carry_ref)

    c = cum + carry_ref[:, 0:1]
    carry_ref[...] = jnp.broadcast_to(c[:, tm - 1:tm], carry_ref.shape)

    pieces = [p.astype(F32) for p in _split3(c * LOG2E)]
    sub = lax.broadcasted_iota(I32, (2 * SUBLANES, tm), 0)
    zeros_tail = jnp.zeros((FOX_KAUG - LANES - 2 * SUBLANES, tm), BF16)
    for p in range(FOX_HEADS // 2):
        ext = jnp.zeros((2 * SUBLANES, tm), F32)
        for hh in range(2):
            for n, piece in enumerate(pieces):
                ext = jnp.where(sub == 3 * hh + n, piece[2 * p + hh:2 * p + hh + 1, :], ext)
        kta_ref[p, 0:LANES, :] = kt[p * LANES:(p + 1) * LANES, :]
        kta_ref[p, LANES:LANES + 2 * SUBLANES, :] = ext.astype(BF16)
        kta_ref[p, LANES + 2 * SUBLANES:, :] = zeros_tail


def _l0_in(x, g, wqv, wkt, wft, bf, wa, wg, *, batch):
    t = x.shape[0]
    tm = TM_IN0
    tpb = (t // batch) // tm
    pairs = FOX_HEADS // 2
    row = lambda w: pl.BlockSpec((tm, w), lambda i: (i, 0))
    consts = [g, wqv, wkt, wft, bf, wa, wg]
    return pl.pallas_call(
        functools.partial(_l0_in_kernel, tiles_per_batch=tpb),
        grid=(t // tm,),
        in_specs=[row(D_MODEL)] + [_const_spec(a.shape) for a in consts],
        out_specs=[row(FOX_WIDTH), pl.BlockSpec((pairs, FOX_KAUG, tm), lambda i: (0, 0, i)),
                   row(FOX_WIDTH), row(CONV_CHANNELS)],
        out_shape=[jax.ShapeDtypeStruct((t, FOX_WIDTH), BF16),
                   jax.ShapeDtypeStruct((pairs, FOX_KAUG, t), BF16),
                   jax.ShapeDtypeStruct((t, FOX_WIDTH), BF16),
                   jax.ShapeDtypeStruct((t, CONV_CHANNELS), F32)],
        scratch_shapes=[pltpu.VMEM((FOX_HEADS, LANES), F32)],
        compiler_params=_cparams(("arbitrary",), 40),
        name="l0_in",
    )(x, *consts)


def _fox_kernel(q_ref, kta_ref, v_ref, o_ref, q_scr, m_scr, acc_scr):
    qi = pl.program_id(2)
    tq = TQ
    tk = TQ
    lane = lax.broadcasted_iota(I32, (tq, LANES), 1)
    is_a = lane < FOX_HEAD_DIM
    lane_k = lax.broadcasted_iota(I32, (tk, LANES), 1)

    ext_a = jnp.where(lane < 3, -1.0, 0.0)
    ext_b = jnp.where((lane >= 3) & (lane < 6), -1.0, 0.0)
    for half in range(FOX_Q_HALVES):
        q2 = q_ref[half * tq:(half + 1) * tq, :].astype(F32)
        q_scr[2 * half] = jnp.concatenate([jnp.where(is_a, q2, 0.0), ext_a], axis=1).astype(BF16)
        q_scr[2 * half + 1] = jnp.concatenate([jnp.where(is_a, 0.0, q2), ext_b], axis=1).astype(BF16)
    m_scr[...] = jnp.full(m_scr.shape, NEG, F32)
    acc_scr[...] = jnp.zeros_like(acc_scr)

    def tile(ki, plan):
        start = pl.multiple_of(ki * tk, tk)
        kt = kta_ref[:, pl.ds(start, tk)]
        vf = v_ref[pl.ds(start, tk), :].astype(F32)
        v_augs = (jnp.where(lane_k < FOX_HEAD_DIM, vf, 1.0).astype(BF16),
                  jnp.where(lane_k < FOX_HEAD_DIM, 1.0, vf).astype(BF16))
        causal = (lax.broadcasted_iota(I32, (tq, tk), 1) <= lax.broadcasted_iota(I32, (tq, tk), 0))
        chains = [(2 * half + h, h, masked) for half, masked in plan for h in range(2)]
        logits = [_dot(q_scr[n], kt) for n, _, _ in chains]
        probs, alphas = [], []
        for (n, _, masked), s in zip(chains, logits):
            if masked:
                s = jnp.where(causal, s, NEG)
            m_old = m_scr[n]
            m_new = jnp.maximum(m_old, jnp.max(s, axis=1, keepdims=True))
            probs.append(jnp.concatenate(
                [jnp.exp2(s[:, j * LANES:(j + 1) * LANES] - m_new).astype(BF16) for j in range(tk // LANES)],
                axis=1))
            alphas.append(jnp.exp2(m_old - m_new))
            m_scr[n] = m_new
        for (n, h, _), p, alpha in zip(chains, probs, alphas):
            acc_scr[n] = acc_scr[n] * alpha + _dot(p, v_augs[h])

    def body(ki, carry):
        tile(ki, tuple((half, False) for half in range(FOX_Q_HALVES)))
        return carry

    first_diag = FOX_Q_HALVES * qi
    lax.fori_loop(0, first_diag, body, 0)
    for d in range(FOX_Q_HALVES):
        tile(first_diag + d, ((d, True),) + tuple((half, False) for half in range(d + 1, FOX_Q_HALVES)))

    for half in range(FOX_Q_HALVES):
        acc_a = acc_scr[2 * half]
        acc_b = acc_scr[2 * half + 1]
        o = jnp.where(is_a, acc_a / pltpu.roll(acc_a, FOX_HEAD_DIM, 1),
                      acc_b / pltpu.roll(acc_b, FOX_HEAD_DIM, 1))
        o_ref[half * tq:(half + 1) * tq, :] = o.astype(o_ref.dtype)


def _fox_attention(q, kta, v, *, batch):
    t = q.shape[0]
    seq = t // batch
    rows = FOX_Q_HALVES * TQ
    nq = seq // rows
    pairs = FOX_HEADS // 2
    nstate = 2 * FOX_Q_HALVES
    return pl.pallas_call(
        _fox_kernel,
        grid=(batch, pairs, nq),
        in_specs=[pl.BlockSpec((rows, LANES), lambda b, p, i: (b * nq + i, p)),
                  pl.BlockSpec((None, FOX_KAUG, seq), lambda b, p, i: (p, 0, b)),
                  pl.BlockSpec((seq, LANES), lambda b, p, i: (b, p))],
        out_specs=pl.BlockSpec((rows, LANES), lambda b, p, i: (b * nq + i, p)),
        out_shape=jax.ShapeDtypeStruct((t, FOX_WIDTH), BF16),
        scratch_shapes=[pltpu.VMEM((nstate, TQ, FOX_KAUG), BF16),
                        pltpu.VMEM((nstate, TQ, LANES), F32), pltpu.VMEM((nstate, TQ, LANES), F32)],
        compiler_params=_cparams(("arbitrary", "arbitrary", "arbitrary"), 40),
        name="fox_attention",
    )(q, kta, v)


def _conv_mod_kernel(u_ref, w_ref, b_ref, g_ref, beta_ref, o_ref, win_ref, sh_ref, *, tiles_per_batch):
    i = pl.program_id(0)
    tm = u_ref.shape[0]

    @pl.when(i % tiles_per_batch == 0)
    def _():
        win_ref[0:CONV_HALO, :] = jnp.zeros((CONV_HALO, CONV_CHANNELS), F32)

    win_ref[CONV_HALO:CONV_HALO + tm, :] = u_ref[...]
    n_sh = sh_ref.shape[1]
    for b in range(1, SUBLANES):
        sh_ref[b - 1] = win_ref[b:b + n_sh, :]
    first = CONV_HALO - (CONV_WIDTH - 1)
    for c in range(tm // CONV_ROWS):
        r0 = c * CONV_ROWS
        acc = jnp.broadcast_to(b_ref[...], (CONV_ROWS, CONV_CHANNELS))
        for j in range(CONV_WIDTH):
            b = (first + j) % SUBLANES
            a = r0 + first + j - b
            tap = win_ref[a:a + CONV_ROWS, :] if b == 0 else sh_ref[b - 1, a:a + CONV_ROWS, :]
            acc = acc + tap * w_ref[j:j + 1, :]
        mu = jnp.mean(acc, axis=-1, keepdims=True)
        d = acc - mu
        var = jnp.mean(d * d, axis=-1, keepdims=True)
        y = d * lax.rsqrt(var + EPS) * g_ref[...] + beta_ref[...]
        o_ref[r0:r0 + CONV_ROWS, :] = (y * _sigmoid(y)).astype(o_ref.dtype)
    win_ref[0:CONV_HALO, :] = win_ref[tm:tm + CONV_HALO, :]


def _conv_module(u, w, b, g, beta, *, batch):
    t = u.shape[0]
    tm = TM_CONV
    tpb = (t // batch) // tm
    return pl.pallas_call(
        functools.partial(_conv_mod_kernel, tiles_per_batch=tpb),
        grid=(t // tm,),
        in_specs=[pl.BlockSpec((tm, CONV_CHANNELS), lambda i: (i, 0)), _const_spec(w.shape),
                  _const_spec(b.shape), _const_spec(g.shape), _const_spec(beta.shape)],
        out_specs=pl.BlockSpec((tm, CONV_CHANNELS), lambda i: (i, 0)),
        out_shape=jax.ShapeDtypeStruct((t, CONV_CHANNELS), BF16),
        scratch_shapes=[pltpu.VMEM((CONV_HALO + tm, CONV_CHANNELS), F32),
                        pltpu.VMEM((SUBLANES - 1, CONV_HALO + tm - SUBLANES, CONV_CHANNELS), F32)],
        compiler_params=_cparams(("arbitrary",), 32),
        name="conv_module",
    )(u, w, b, g, beta)


def _l0_out_ffn_kernel(x_ref, att_ref, u_ref, woa_ref, wou_ref, g_ref, wg_ref, wu_ref, wd_ref, o_ref):
    x1 = x_ref[...] + _dot(att_ref[...], woa_ref[...]) + _dot(u_ref[...], wou_ref[...])
    hb = _rms(x1, g_ref[...]).astype(BF16)
    gate = _dot(hb, wg_ref[...])
    act = (gate * _sigmoid(gate) * _dot(hb, wu_ref[...])).astype(BF16)
    o_ref[...] = x1 + _dot(act, wd_ref[...])


def _l0_out_ffn(x, att, u, woa, wou, g, wg, wu, wd):
    t = x.shape[0]
    tm = TM_FFN0
    row = lambda w: pl.BlockSpec((tm, w), lambda i: (i, 0))
    return pl.pallas_call(
        _l0_out_ffn_kernel,
        grid=(t // tm,),
        in_specs=[row(D_MODEL), row(FOX_WIDTH), row(CONV_CHANNELS), _const_spec(woa.shape),
                  _const_spec(wou.shape), _const_spec(g.shape), _const_spec(wg.shape),
                  _const_spec(wu.shape), _const_spec(wd.shape)],
        out_specs=row(D_MODEL),
        out_shape=jax.ShapeDtypeStruct((t, D_MODEL), F32),
        compiler_params=_cparams(("arbitrary",), 56),
        name="l0_out_ffn",
    )(x, att, u, woa, wou, g, wg, wu, wd)


def _l1_in_kernel(x_ref, g_ref, wuvo_ref, wfc_ref, bfc_ref, wgr_ref, bgr_ref, cw_ref, cb_ref,
                  wq_ref, wkt_ref, q_ref, kt_ref, v_ref, og_ref, lfc_ref, gr_ref, win_ref,
                  *, tiles_per_batch):
    i = pl.program_id(0)
    tm = x_ref.shape[0]
    w = MLSTM_WIDTH
    hb = _rms(x_ref[...], g_ref[...]).astype(BF16)
    uvo = _dot(hb, wuvo_ref[...])
    v_ref[...] = uvo[:, w:2 * w].astype(BF16)
    og_ref[...] = _sigmoid(uvo[:, 2 * w:]).astype(BF16)

    lfc_ref[...] = _log_sigmoid(_dot(hb, wfc_ref[...]) + bfc_ref[...])
    grow = _dot_nt(wgr_ref[...], hb) + bgr_ref[...]
    is_i = lax.broadcasted_iota(I32, grow.shape, 0) < MLSTM_HEADS
    gr_ref[...] = jnp.where(is_i, grow, _log_sigmoid(grow))

    @pl.when(i % tiles_per_batch == 0)
    def _():
        win_ref[0:L1_HALO, :] = jnp.zeros((L1_HALO, w), F32)

    win_ref[L1_HALO:L1_HALO + tm, :] = uvo[:, :w]
    first = L1_HALO - (MLSTM_CONV_WIDTH - 1)
    acc = jnp.broadcast_to(cb_ref[...], (tm, w))
    for j in range(MLSTM_CONV_WIDTH):
        acc = acc + win_ref[first + j:first + j + tm, :] * cw_ref[j:j + 1, :]
    win_ref[0:L1_HALO, :] = win_ref[tm:tm + L1_HALO, :]
    uc = (acc * _sigmoid(acc)).astype(BF16)
    d = MLSTM_HEAD_DIM
    for h in range(MLSTM_HEADS):
        uh = uc[:, h * d:(h + 1) * d]
        q_ref[:, h * d:(h + 1) * d] = _dot(uh, wq_ref[h]).astype(BF16)
        kt_ref[h * d:(h + 1) * d, :] = (_dot_nt(wkt_ref[h], uh) * (d ** -0.5)).astype(BF16)


def _l1_in(x, g, wuvo, wfc, bfc, wgr, bgr, cw, cb, wq, wkt, *, batch):
    t = x.shape[0]
    tm = TM_IN1
    tpb = (t // batch) // tm
    w = MLSTM_WIDTH
    row = lambda n: pl.BlockSpec((tm, n), lambda i: (i, 0))
    col = lambda n: pl.BlockSpec((n, tm), lambda i: (0, i))
    consts = [g, wuvo, wfc, bfc, wgr, bgr, cw, cb, wq, wkt]
    return pl.pallas_call(
        functools.partial(_l1_in_kernel, tiles_per_batch=tpb),
        grid=(t // tm,),
        in_specs=[row(D_MODEL)] + [_const_spec(a.shape) for a in consts],
        out_specs=[row(w), col(w), row(w), row(w), row(LANES), col(2 * MLSTM_HEADS)],
        out_shape=[jax.ShapeDtypeStruct((t, w), BF16), jax.ShapeDtypeStruct((w, t), BF16),
                   jax.ShapeDtypeStruct((t, w), BF16), jax.ShapeDtypeStruct((t, w), BF16),
                   jax.ShapeDtypeStruct((t, LANES), F32),
                   jax.ShapeDtypeStruct((2 * MLSTM_HEADS, t), F32)],
        scratch_shapes=[pltpu.VMEM((L1_HALO + tm, w), F32)],
        compiler_params=_cparams(("arbitrary",), 48),
        name="l1_in",
    )(x, *consts)


def _mlstm_kernel(q_ref, kt_ref, v_ref, og_ref, lfc_ref, gr_ref, hg_ref, o_ref, c_scr, m_scr):
    ci = pl.program_id(1)
    L = q_ref.shape[0]
    d = MLSTM_HEAD_DIM
    nh = MLSTM_HEADS

    @pl.when(ci == 0)
    def _():
        c_scr[...] = jnp.zeros_like(c_scr)
        m_scr[...] = jnp.zeros_like(m_scr)

    r = lax.broadcasted_iota(I32, (L, L), 0)
    c = lax.broadcasted_iota(I32, (L, L), 1)
    tril = r >= c
    tri_lo = jnp.where(tril, 1.0, 0.0).astype(BF16)
    tri_up = jnp.where(r <= c, 1.0, 0.0).astype(BF16)

    bc_col = sum(_dot(tri_lo, p) for p in _split3(lfc_ref[...]))
    bc_row = sum(_dot(p, tri_up) for p in _split3(gr_ref[nh:2 * nh, :]))
    ones_blk = jnp.ones((L, d), BF16)
    heads = range(nh)
    sls = [slice(h * d, (h + 1) * d) for h in heads]

    cts = [c_scr[h] for h in heads]
    v_augs = [jnp.concatenate([v_ref[:, sls[h]], ones_blk], axis=1) for h in heads]
    qk = [_dot(q_ref[:, sls[h]], kt_ref[sls[h], :]) for h in heads]
    qc = [_dot(q_ref[:, sls[h]], cts[h].astype(BF16)) for h in heads]

    w_intra, w_inter, m_ts, new_state = [], [], [], []
    for h in heads:
        bcol = bc_col[:, h:h + 1]
        brow = bc_row[h:h + 1, :]
        srow = gr_ref[h:h + 1, :] - brow
        g = brow[:, L - 1:L]
        m_old = m_scr[h][:, 0:1]
        dmat = jnp.where(tril, bcol + srow, NEG)
        inter = bcol + m_old
        m_t = jnp.maximum(inter, jnp.max(dmat, axis=1, keepdims=True))
        w_inter.append(jnp.exp(inter - m_t))
        w_intra.append((jnp.exp(dmat - m_t) * qk[h]).astype(BF16))
        m_ts.append(m_t)
        a_row = g + srow
        m_new = jnp.maximum(g + m_old, jnp.max(a_row, axis=1, keepdims=True))
        kw = (kt_ref[sls[h], :].astype(F32) * jnp.exp(a_row - m_new)).astype(BF16)
        new_state.append((jnp.exp(g + m_old - m_new), kw, m_new))

    for h in heads:
        res = w_inter[h] * qc[h] + _dot(w_intra[h], v_augs[h])
        den = jnp.maximum(jnp.abs(res[:, d:]), jnp.exp(-m_ts[h]))
        hc = og_ref[:, sls[h]].astype(F32) * (res[:, :d] / den)
        o_ref[:, sls[h]] = _rms(hc, hg_ref[:, sls[h]]).astype(o_ref.dtype)

    for h in heads:
        decay, kw, m_new = new_state[h]
        c_scr[h] = decay * cts[h] + _dot(kw, v_augs[h])
        m_scr[h] = jnp.broadcast_to(m_new, (1, LANES))


def _mlstm(q, kt, v, og, lfc, gr, hg, *, batch):
    t = q.shape[0]
    w = MLSTM_WIDTH
    nc = (t // batch) // CHUNK
    row = lambda n: pl.BlockSpec((CHUNK, n), lambda b, c: (b * nc + c, 0))
    col = lambda n: pl.BlockSpec((n, CHUNK), lambda b, c: (0, b * nc + c))
    return pl.pallas_call(
        _mlstm_kernel,
        grid=(batch, nc),
        in_specs=[row(w), col(w), row(w), row(w), row(LANES), col(2 * MLSTM_HEADS), _const_spec(hg.shape)],
        out_specs=row(w),
        out_shape=jax.ShapeDtypeStruct((t, w), BF16),
        scratch_shapes=[pltpu.VMEM((MLSTM_HEADS, MLSTM_HEAD_DIM, 2 * MLSTM_HEAD_DIM), F32),
                        pltpu.VMEM((MLSTM_HEADS, 1, LANES), F32)],
        compiler_params=_cparams(("arbitrary", "arbitrary"), 32),
        name="mlstm",
    )(q, kt, v, og, lfc, gr, hg)


def _l1_out_router_kernel(x_ref, hc_ref, wo_ref, g_ref, wr_ref, x3_ref, h3_ref, meta_ref, cnt_ref, carry_ref):
    i = pl.program_id(0)
    tm = x_ref.shape[0]

    @pl.when(i == 0)
    def _():
        carry_ref[...] = jnp.zeros_like(carry_ref)

    x3 = x_ref[...] + _dot(hc_ref[...], wo_ref[...])
    x3_ref[...] = x3
    h3 = _rms(x3, g_ref[...])
    for s in range(D_MODEL // LANES):
        h3_ref[pl.ds(s, tm, stride=D_MODEL // LANES), :] = h3[:, s * LANES:(s + 1) * LANES]

    h_hi, h_mid, _ = _split3(h3)
    w_hi, w_mid, _ = _split3(wr_ref[...])
    logits = _dot(h_hi, w_hi) + (_dot(h_hi, w_mid) + _dot(h_mid, w_hi))

    lane = lax.broadcasted_iota(I32, (tm, LANES), 1)
    lane_f = lane.astype(F32)
    lg = jnp.where(lane < N_EXPERTS, logits, NEG)
    v1 = jnp.max(lg, axis=1, keepdims=True)
    i1 = jnp.min(jnp.where(lg == v1, lane_f, float(LANES)), axis=1, keepdims=True)
    lg2 = jnp.where(lane_f == i1, NEG, lg)
    v2 = jnp.max(lg2, axis=1, keepdims=True)
    i2 = jnp.min(jnp.where(lg2 == v2, lane_f, float(LANES)), axis=1, keepdims=True)
    e = jnp.exp(v2 - v1)
    g1 = 1.0 / (1.0 + e)
    g2 = e / (1.0 + e)

    oh1 = lane_f == i1
    oh2 = lane_f == i2
    oh = jnp.where(oh1 | oh2, 1.0, 0.0)
    r = lax.broadcasted_iota(I32, (tm, tm), 0)
    c = lax.broadcasted_iota(I32, (tm, tm), 1)
    strict = jnp.where(c < r, 1.0, 0.0).astype(BF16)
    pos = _dot(strict, oh.astype(BF16)) + carry_ref[0:1, :]
    rank1 = jnp.sum(jnp.where(oh1, pos, 0.0), axis=1, keepdims=True)
    rank2 = jnp.sum(jnp.where(oh2, pos, 0.0), axis=1, keepdims=True)
    total = carry_ref[0:1, :] + jnp.sum(oh, axis=0, keepdims=True)
    carry_ref[...] = jnp.broadcast_to(total, carry_ref.shape)
    cnt_ref[...] = jnp.broadcast_to(total, cnt_ref.shape)

    meta = jnp.zeros((tm, LANES), F32)
    for k, val in enumerate((i1, i2, g1, g2, rank1, rank2)):
        meta = jnp.where(lane == k, val, meta)
    meta_ref[...] = meta


def _l1_out_router(x, hc, wo, g, wr):
    t = x.shape[0]
    tm = TM_OUT1
    groups = D_MODEL // LANES
    row = lambda n: pl.BlockSpec((tm, n), lambda i: (i, 0))
    return pl.pallas_call(
        _l1_out_router_kernel,
        grid=(t // tm,),
        in_specs=[row(D_MODEL), row(MLSTM_WIDTH), _const_spec(wo.shape), _const_spec(g.shape),
                  _const_spec(wr.shape)],
        out_specs=[row(D_MODEL), pl.BlockSpec((tm * groups, LANES), lambda i: (i, 0)), row(LANES),
                   pl.BlockSpec((SUBLANES, LANES), lambda i: (0, 0))],
        out_shape=[jax.ShapeDtypeStruct((t, D_MODEL), F32), jax.ShapeDtypeStruct((t * groups, LANES), F32),
                   jax.ShapeDtypeStruct((t, LANES), F32), jax.ShapeDtypeStruct((SUBLANES, LANES), F32)],
        scratch_shapes=[pltpu.VMEM((SUBLANES, LANES), F32)],
        compiler_params=_cparams(("arbitrary",), 32),
        name="l1_out_router",
    )(x, hc, wo, g, wr)


def _gather_rows(tab0_ref, tabn_ref, src_hbm, buf, sem, group=1):
    i = pl.program_id(0)
    n = pl.num_programs(0)
    rows = buf.shape[1] // group
    slot = i % 2

    def row_copy(s, item, r):
        src_row = item * group
        if group > 1 and not isinstance(item, int):
            src_row = pl.multiple_of(src_row, group)
        return pltpu.make_async_copy(src_hbm.at[pl.ds(src_row, group), :],
                                     buf.at[s, pl.ds(r * group, group), :], sem.at[s])

    @pl.when(i == 0)
    def _():
        def start(r, c):
            row_copy(0, tab0_ref[0, r], r).start()
            return c
        lax.fori_loop(0, rows, start, 0, unroll=8)

    @pl.when(i + 1 < n)
    def _():
        for r in range(rows):
            row_copy(1 - slot, tabn_ref[0, r], r).start()

    def wait(r, c):
        row_copy(slot, 0, r).wait()
        return c
    lax.fori_loop(0, rows, wait, 0, unroll=8)
    return slot


def _gather_specs(table, n):
    width = table.shape[2]

    def smem_tile(index_map):
        return pl.BlockSpec((None, 1, width), index_map, memory_space=pltpu.SMEM)

    return [smem_tile(lambda i: (0, 0, 0)), smem_tile(lambda i: (jnp.minimum(i + 1, n - 1), 0, 0))]


def _dispatch_kernel(tab0_ref, tabn_ref, h_hbm, o_ref, buf, sem):
    groups = D_MODEL // LANES
    rows = o_ref.shape[0]
    slot = _gather_rows(tab0_ref, tabn_ref, h_hbm, buf, sem, group=groups)
    for s in range(groups):
        o_ref[:, s * LANES:(s + 1) * LANES] = buf[slot, pl.ds(s, rows, stride=groups), :].astype(o_ref.dtype)


def _dispatch(src, h3):
    nt = src.shape[0]
    groups = D_MODEL // LANES
    return pl.pallas_call(
        _dispatch_kernel,
        grid=(nt,),
        in_specs=_gather_specs(src, nt) + [pl.BlockSpec(memory_space=pl.ANY)],
        out_specs=pl.BlockSpec((TM_DISPATCH, D_MODEL), lambda i: (i, 0)),
        out_shape=jax.ShapeDtypeStruct((nt * TM_DISPATCH, D_MODEL), BF16),
        scratch_shapes=[pltpu.VMEM((2, TM_DISPATCH * groups, LANES), F32), pltpu.SemaphoreType.DMA((2,))],
        compiler_params=_cparams(("arbitrary",), 48),
        name="dispatch",
    )(src, src, h3)


def _experts_kernel(te_ref, nused_ref, x_ref, wg_ref, wu_ref, wd_ref, o_ref):
    i = pl.program_id(0)
    j = pl.program_id(1)

    @pl.when(j == 0)
    def _():
        o_ref[...] = jnp.zeros_like(o_ref)

    @pl.when(i < nused_ref[0])
    def _():
        xb = x_ref[...]
        gate = _dot(xb, wg_ref[...])
        act = (gate * _sigmoid(gate) * _dot(xb, wu_ref[...])).astype(BF16)
        o_ref[...] += _dot(act, wd_ref[...])


def _experts(tile_expert, nused, xs, wg, wu, wd):
    nt = xs.shape[0] // TM_E
    nf = D_FF_EXPERT // TF_E

    def wcol(i, j, te, nu):
        return (te[i], 0, jnp.where(i < nu[0], j, nf - 1))

    def wrow(i, j, te, nu):
        return (te[i], jnp.where(i < nu[0], j, nf - 1), 0)

    grid_spec = pltpu.PrefetchScalarGridSpec(
        num_scalar_prefetch=2,
        grid=(nt, nf),
        in_specs=[pl.BlockSpec((TM_E, D_MODEL), lambda i, j, te, nu: (i, 0)),
                  pl.BlockSpec((None, D_MODEL, TF_E), wcol), pl.BlockSpec((None, D_MODEL, TF_E), wcol),
                  pl.BlockSpec((None, TF_E, D_MODEL), wrow)],
        out_specs=pl.BlockSpec((TM_E, D_MODEL), lambda i, j, te, nu: (i, 0)),
    )
    return pl.pallas_call(
        _experts_kernel,
        grid_spec=grid_spec,
        out_shape=jax.ShapeDtypeStruct((nt * TM_E, D_MODEL), F32),
        compiler_params=_cparams(("arbitrary", "arbitrary"), 48),
        name="experts",
    )(tile_expert, nused, xs, wg, wu, wd)


def _combine_kernel(pos0_ref, posn_ref, x_ref, meta_ref, g_ref, y_hbm, o_ref, y_scr, sem):
    tm = x_ref.shape[0]
    slot = _gather_rows(pos0_ref, posn_ref, y_hbm, y_scr, sem)
    meta = meta_ref[...]
    y = x_ref[...] + meta[:, 2:3] * y_scr[slot, 0:tm, :] + meta[:, 3:4] * y_scr[slot, tm:2 * tm, :]
    o_ref[...] = _rms(y, g_ref[...])


def _combine(pos, x3, y, meta, g):
    t = x3.shape[0]
    tm = TM_FIN
    nt = t // tm
    row = lambda n: pl.BlockSpec((tm, n), lambda i: (i, 0))
    return pl.pallas_call(
        _combine_kernel,
        grid=(nt,),
        in_specs=_gather_specs(pos, nt) + [row(D_MODEL), row(LANES), _const_spec(g.shape),
                                           pl.BlockSpec(memory_space=pl.ANY)],
        out_specs=row(D_MODEL),
        out_shape=jax.ShapeDtypeStruct((t, D_MODEL), F32),
        scratch_shapes=[pltpu.VMEM((2, 2 * tm, D_MODEL), F32), pltpu.SemaphoreType.DMA((2,))],
        compiler_params=_cparams(("arbitrary",), 56),
        name="combine_final_norm",
    )(pos, pos, x3, meta, g, y)


def _route(meta, cnt, t):
    idx = meta[:, 0:2].astype(I32)
    rank = meta[:, 4:6].astype(I32)
    counts = cnt[0, :N_EXPERTS].astype(I32)
    tiles_e = (counts + TM_E - 1) // TM_E
    tile_end = jnp.cumsum(tiles_e)
    row_off = (tile_end - tiles_e) * TM_E
    nt = 2 * t // TM_E + N_EXPERTS
    p = nt * TM_E
    pos = row_off[idx] + rank
    src = jnp.zeros((p,), I32).at[pos.reshape(-1)].set(jnp.repeat(jnp.arange(t, dtype=I32), 2),
                                                       unique_indices=True)
    src = src.reshape(p // TM_DISPATCH, 1, TM_DISPATCH)
    tiles = jnp.arange(nt, dtype=I32)
    tile_expert = jnp.minimum(jnp.sum((tiles[:, None] >= tile_end[None, :]).astype(I32), axis=1),
                              N_EXPERTS - 1)
    nused = tile_end[-1:].astype(I32)
    pos_tiles = pos.reshape(t // TM_FIN, TM_FIN, 2).transpose(0, 2, 1).reshape(t // TM_FIN, 1, 2 * TM_FIN)
    return tile_expert, nused, src, pos_tiles


def kernel(x, l0_norm1_g, l0_w_in, l0_b_f, l0_dw_w, l0_dw_b, l0_cln_g, l0_cln_b, l0_w_out, l0_norm2_g, l0_w_gate, l0_w_up, l0_w_down, l1_norm1_g, l1_w_in, l1_b_i, l1_b_f, l1_conv_w, l1_conv_b, l1_wq_head, l1_wk_head, l1_hnorm_g, l1_w_out, l1_norm2_g, l1_router, l1_e_gate, l1_e_up, l1_e_down, final_norm_g):
    batch, seq, d = x.shape
    t = batch * seq
    xf = x.reshape(t, d)
    vec = lambda a: a.reshape(1, -1).astype(F32)
    bf = lambda a: a.astype(BF16)

    fw = FOX_WIDTH
    o_f = 3 * fw
    o_a = o_f + FOX_HEADS
    o_g = o_a + CONV_CHANNELS
    w_qv = jnp.concatenate([l0_w_in[:, :fw], l0_w_in[:, 2 * fw:o_f]], axis=1)
    q, kta, v, u = _l0_in(
        xf, vec(l0_norm1_g), bf(w_qv), bf(l0_w_in[:, fw:2 * fw].T), bf(l0_w_in[:, o_f:o_a].T),
        l0_b_f.reshape(-1, 1).astype(F32), bf(l0_w_in[:, o_a:o_g]), bf(l0_w_in[:, o_g:]), batch=batch)
    att = _fox_attention(q, kta, v, batch=batch)
    dw_w = jnp.concatenate([l0_dw_w, jnp.zeros((1, CONV_CHANNELS), F32)], axis=0)
    uc = _conv_module(u, dw_w, vec(l0_dw_b), vec(l0_cln_g), vec(l0_cln_b), batch=batch)
    x2 = _l0_out_ffn(xf, att, uc, bf(l0_w_out[:fw]), bf(l0_w_out[fw:]), vec(l0_norm2_g),
                     bf(l0_w_gate), bf(l0_w_up), bf(l0_w_down))

    w = MLSTM_WIDTH
    nh = MLSTM_HEADS
    w_i = l1_w_in[:, 3 * w:3 * w + nh]
    w_f = l1_w_in[:, 3 * w + nh:]
    wfc = bf(jnp.pad(w_f, ((0, 0), (0, LANES - nh))))
    bfc = jnp.pad(l1_b_f, (0, LANES - nh)).reshape(1, LANES).astype(F32)
    wgr = bf(jnp.concatenate([w_i.T, w_f.T], axis=0))
    bgr = jnp.concatenate([l1_b_i, l1_b_f]).reshape(-1, 1).astype(F32)
    ql, kt, vl, og, lfc, gr = _l1_in(
        x2, vec(l1_norm1_g), bf(l1_w_in[:, :3 * w]), wfc, bfc, wgr, bgr, l1_conv_w.astype(F32),
        vec(l1_conv_b), bf(l1_wq_head), bf(jnp.swapaxes(l1_wk_head, 1, 2)), batch=batch)
    hc = _mlstm(ql, kt, vl, og, lfc, gr, vec(l1_hnorm_g), batch=batch)
    wr = jnp.pad(l1_router.astype(F32), ((0, 0), (0, LANES - N_EXPERTS)))
    x3, h3, meta, cnt = _l1_out_router(x2, hc, bf(l1_w_out), vec(l1_norm2_g), wr)

    tile_expert, nused, src, pos_tiles = _route(meta, cnt, t)
    xs = _dispatch(src, h3)
    y = _experts(tile_expert, nused, xs, bf(l1_e_gate), bf(l1_e_up), bf(l1_e_down))
    out = _combine(pos_tiles, x3, y, meta, vec(final_norm_g))
    return out.reshape(batch, seq, d)
```

```python
import functools

import jax
import jax.numpy as jnp
from jax import lax
from jax.experimental import pallas as pl
from jax.experimental.pallas import tpu as pltpu

F32 = jnp.float32
BF16 = jnp.bfloat16
I32 = jnp.int32

EPS = 1e-6
NEG = -1e30
LOG2E = 1.4426950408889634

D_MODEL = 1024
FOX_HEADS = 8
FOX_HEAD_DIM = 64
FOX_WIDTH = FOX_HEADS * FOX_HEAD_DIM
CONV_CHANNELS = 512
CONV_WIDTH = 31
MLSTM_HEADS = 8
MLSTM_HEAD_DIM = 128
MLSTM_WIDTH = MLSTM_HEADS * MLSTM_HEAD_DIM
MLSTM_CONV_WIDTH = 4
D_FF = 2816
N_EXPERTS = 8
D_FF_EXPERT = 3584

LANES = 128
SUBLANES = 8
MIB = 1024 * 1024

TM_IN0 = 512
TQ = 512
FOX_Q_HALVES = 4
FOX_KAUG = 256
TM_CONV = 512
CONV_ROWS = 64
CONV_HALO = 32
TM_FFN0 = 512
TM_IN1 = 512
L1_HALO = 8
CHUNK = 256
TM_OUT1 = 512
TM_E = 512
TF_E = 512
TM_DISPATCH = 512
TM_FIN = 512


def _cparams(sem, vmem_mib):
    return pltpu.CompilerParams(dimension_semantics=sem, vmem_limit_bytes=vmem_mib * MIB)


def _rms(x, g):
    return x * lax.rsqrt(jnp.mean(x * x, axis=-1, keepdims=True) + EPS) * g


def _sigmoid(x):
    return 1.0 / (1.0 + jnp.exp(-x))


def _log_sigmoid(x):
    return jnp.minimum(x, 0.0) - jnp.log(1.0 + jnp.exp(-jnp.abs(x)))


def _dot(a, b):
    return jnp.dot(a, b, preferred_element_type=F32)


def _dot_nt(a, b):
    return lax.dot_general(a, b, (((1,), (1,)), ((), ())), preferred_element_type=F32)


def _split3(x):
    hi = x.astype(BF16)
    r1 = x - hi.astype(F32)
    mid = r1.astype(BF16)
    lo = (r1 - mid.astype(F32)).astype(BF16)
    return hi, mid, lo


def _const_spec(shape):
    nd = len(shape)
    return pl.BlockSpec(shape, lambda *_: (0,) * nd, pipeline_mode=pl.Buffered(1))


def _l0_in_kernel(x_ref, g_ref, wqv_ref, wkt_ref, wft_ref, bf_ref, wa_ref, wg_ref,
                  q_ref, kta_ref, v_ref, u_ref, carry_ref, *, tiles_per_batch):
    i = pl.program_id(0)
    tm = x_ref.shape[0]
    hb = _rms(x_ref[...], g_ref[...]).astype(BF16)
    qv = _dot(hb, wqv_ref[...])
    q_ref[...] = (qv[:, :FOX_WIDTH] * (FOX_HEAD_DIM ** -0.5 * LOG2E)).astype(BF16)
    v_ref[...] = qv[:, FOX_WIDTH:].astype(BF16)
    kt = _dot_nt(wkt_ref[...], hb).astype(BF16)
    u_ref[...] = _dot(hb, wa_ref[...]) * _sigmoid(_dot(hb, wg_ref[...]))

    logf = _log_sigmoid(_dot_nt(wft_ref[...], hb) + bf_ref[...])
    lane = lax.broadcasted_iota(I32, logf.shape, 1)
    cum = logf
    shift = 1
    while shift < tm:
        cum = cum + jnp.where(lane >= shift, pltpu.roll(cum, shift, 1), 0.0)
        shift *= 2

    @pl.when(i % tiles_per_batch == 0)
    def _():
        carry_ref[...] = jnp.zeros_like(carry_ref)

    c = cum + carry_ref[:, 0:1]
    carry_ref[...] = jnp.broadcast_to(c[:, tm - 1:tm], carry_ref.shape)

    pieces = [p.astype(F32) for p in _split3(c * LOG2E)]
    sub = lax.broadcasted_iota(I32, (2 * SUBLANES, tm), 0)
    zeros_tail = jnp.zeros((FOX_KAUG - LANES - 2 * SUBLANES, tm), BF16)
    for p in range(FOX_HEADS // 2):
        ext = jnp.zeros((2 * SUBLANES, tm), F32)
        for hh in range(2):
            for n, piece in enumerate(pieces):
                ext = jnp.where(sub == 3 * hh + n, piece[2 * p + hh:2 * p + hh + 1, :], ext)
        kta_ref[p, 0:LANES, :] = kt[p * LANES:(p + 1) * LANES, :]
        kta_ref[p, LANES:LANES + 2 * SUBLANES, :] = ext.astype(BF16)
        kta_ref[p, LANES + 2 * SUBLANES:, :] = zeros_tail


def _l0_in(x, g, wqv, wkt, wft, bf, wa, wg, *, batch):
    t = x.shape[0]
    tm = TM_IN0
    tpb = (t // batch) // tm
    pairs = FOX_HEADS // 2
    row = lambda w: pl.BlockSpec((tm, w), lambda i: (i, 0))
    consts = [g, wqv, wkt, wft, bf, wa, wg]
    return pl.pallas_call(
        functools.partial(_l0_in_kernel, tiles_per_batch=tpb),
        grid=(t // tm,),
        in_specs=[row(D_MODEL)] + [_const_spec(a.shape) for a in consts],
        out_specs=[row(FOX_WIDTH), pl.BlockSpec((pairs, FOX_KAUG, tm), lambda i: (0, 0, i)),
                   row(FOX_WIDTH), row(CONV_CHANNELS)],
        out_shape=[jax.ShapeDtypeStruct((t, FOX_WIDTH), BF16),
                   jax.ShapeDtypeStruct((pairs, FOX_KAUG, t), BF16),
                   jax.ShapeDtypeStruct((t, FOX_WIDTH), BF16),
                   jax.ShapeDtypeStruct((t, CONV_CHANNELS), F32)],
        scratch_shapes=[pltpu.VMEM((FOX_HEADS, LANES), F32)],
        compiler_params=_cparams(("arbitrary",), 40),
        name="l0_in",
    )(x, *consts)


def _fox_kernel(q_ref, kta_ref, v_ref, o_ref, q_scr, m_scr, acc_scr):
    qi = pl.program_id(2)
    tq = TQ
    tk = TQ
    lane = lax.broadcasted_iota(I32, (tq, LANES), 1)
    is_a = lane < FOX_HEAD_DIM
    lane_k = lax.broadcasted_iota(I32, (tk, LANES), 1)

    ext_a = jnp.where(lane < 3, -1.0, 0.0)
    ext_b = jnp.where((lane >= 3) & (lane < 6), -1.0, 0.0)
    for half in range(FOX_Q_HALVES):
        q2 = q_ref[half * tq:(half + 1) * tq, :].astype(F32)
        q_scr[2 * half] = jnp.concatenate([jnp.where(is_a, q2, 0.0), ext_a], axis=1).astype(BF16)
        q_scr[2 * half + 1] = jnp.concatenate([jnp.where(is_a, 0.0, q2), ext_b], axis=1).astype(BF16)
    m_scr[...] = jnp.full(m_scr.shape, NEG, F32)
    acc_scr[...] = jnp.zeros_like(acc_scr)

    def tile(ki, plan):
        start = pl.multiple_of(ki * tk, tk)
        kt = kta_ref[:, pl.ds(start, tk)]
        vf = v_ref[pl.ds(start, tk), :].astype(F32)
        v_augs = (jnp.where(lane_k < FOX_HEAD_DIM, vf, 1.0).astype(BF16),
                  jnp.where(lane_k < FOX_HEAD_DIM, 1.0, vf).astype(BF16))
        causal = (lax.broadcasted_iota(I32, (tq, tk), 1) <= lax.broadcasted_iota(I32, (tq, tk), 0))
        chains = [(2 * half + h, h, masked) for half, masked in plan for h in range(2)]
        logits = [_dot(q_scr[n], kt) for n, _, _ in chains]
        probs, alphas = [], []
        for (n, _, masked), s in zip(chains, logits):
            if masked:
                s = jnp.where(causal, s, NEG)
            m_old = m_scr[n]
            m_new = jnp.maximum(m_old, jnp.max(s, axis=1, keepdims=True))
            probs.append(jnp.concatenate(
                [jnp.exp2(s[:, j * LANES:(j + 1) * LANES] - m_new).astype(BF16) for j in range(tk // LANES)],
                axis=1))
            alphas.append(jnp.exp2(m_old - m_new))
            m_scr[n] = m_new
        for (n, h, _), p, alpha in zip(chains, probs, alphas):
            acc_scr[n] = acc_scr[n] * alpha + _dot(p, v_augs[h])

    def body(ki, carry):
        tile(ki, tuple((half, False) for half in range(FOX_Q_HALVES)))
        return carry

    first_diag = FOX_Q_HALVES * qi
    lax.fori_loop(0, first_diag, body, 0)
    for d in range(FOX_Q_HALVES):
        tile(first_diag + d, ((d, True),) + tuple((half, False) for half in range(d + 1, FOX_Q_HALVES)))

    for half in range(FOX_Q_HALVES):
        acc_a = acc_scr[2 * half]
        acc_b = acc_scr[2 * half + 1]
        o = jnp.where(is_a, acc_a / pltpu.roll(acc_a, FOX_HEAD_DIM, 1),
                      acc_b / pltpu.roll(acc_b, FOX_HEAD_DIM, 1))
        o_ref[half * tq:(half + 1) * tq, :] = o.astype(o_ref.dtype)


def _fox_attention(q, kta, v, *, batch):
    t = q.shape[0]
    seq = t // batch
    rows = FOX_Q_HALVES * TQ
    nq = seq // rows
    pairs = FOX_HEADS // 2
    nstate = 2 * FOX_Q_HALVES
    return pl.pallas_call(
        _fox_kernel,
        grid=(batch, pairs, nq),
        in_specs=[pl.BlockSpec((rows, LANES), lambda b, p, i: (b * nq + i, p)),
                  pl.BlockSpec((None, FOX_KAUG, seq), lambda b, p, i: (p, 0, b)),
                  pl.BlockSpec((seq, LANES), lambda b, p, i: (b, p))],
        out_specs=pl.BlockSpec((rows, LANES), lambda b, p, i: (b * nq + i, p)),
        out_shape=jax.ShapeDtypeStruct((t, FOX_WIDTH), BF16),
        scratch_shapes=[pltpu.VMEM((nstate, TQ, FOX_KAUG), BF16),
                        pltpu.VMEM((nstate, TQ, LANES), F32), pltpu.VMEM((nstate, TQ, LANES), F32)],
        compiler_params=_cparams(("arbitrary", "arbitrary", "arbitrary"), 40),
        name="fox_attention",
    )(q, kta, v)


def _conv_mod_kernel(u_ref, w_ref, b_ref, g_ref, beta_ref, o_ref, win_ref, sh_ref, *, tiles_per_batch):
    i = pl.program_id(0)
    tm = u_ref.shape[0]

    @pl.when(i % tiles_per_batch == 0)
    def _():
        win_ref[0:CONV_HALO, :] = jnp.zeros((CONV_HALO, CONV_CHANNELS), F32)

    win_ref[CONV_HALO:CONV_HALO + tm, :] = u_ref[...]
    n_sh = sh_ref.shape[1]
    for b in range(1, SUBLANES):
        sh_ref[b - 1] = win_ref[b:b + n_sh, :]
    first = CONV_HALO - (CONV_WIDTH - 1)
    for c in range(tm // CONV_ROWS):
        r0 = c * CONV_ROWS
        acc = jnp.broadcast_to(b_ref[...], (CONV_ROWS, CONV_CHANNELS))
        for j in range(CONV_WIDTH):
            b = (first + j) % SUBLANES
            a = r0 + first + j - b
            tap = win_ref[a:a + CONV_ROWS, :] if b == 0 else sh_ref[b - 1, a:a + CONV_ROWS, :]
            acc = acc + tap * w_ref[j:j + 1, :]
        mu = jnp.mean(acc, axis=-1, keepdims=True)
        d = acc - mu
        var = jnp.mean(d * d, axis=-1, keepdims=True)
        y = d * lax.rsqrt(var + EPS) * g_ref[...] + beta_ref[...]
        o_ref[r0:r0 + CONV_ROWS, :] = (y * _sigmoid(y)).astype(o_ref.dtype)
    win_ref[0:CONV_HALO, :] = win_ref[tm:tm + CONV_HALO, :]


def _conv_module(u, w, b, g, beta, *, batch):
    t = u.shape[0]
    tm = TM_CONV
    tpb = (t // batch) // tm
    return pl.pallas_call(
        functools.partial(_conv_mod_kernel, tiles_per_batch=tpb),
        grid=(t // tm,),
        in_specs=[pl.BlockSpec((tm, CONV_CHANNELS), lambda i: (i, 0)), _const_spec(w.shape),
                  _const_spec(b.shape), _const_spec(g.shape), _const_spec(beta.shape)],
        out_specs=pl.BlockSpec((tm, CONV_CHANNELS), lambda i: (i, 0)),
        out_shape=jax.ShapeDtypeStruct((t, CONV_CHANNELS), BF16),
        scratch_shapes=[pltpu.VMEM((CONV_HALO + tm, CONV_CHANNELS), F32),
                        pltpu.VMEM((SUBLANES - 1, CONV_HALO + tm - SUBLANES, CONV_CHANNELS), F32)],
        compiler_params=_cparams(("arbitrary",), 32),
        name="conv_module",
    )(u, w, b, g, beta)


def _l0_out_ffn_kernel(x_ref, att_ref, u_ref, woa_ref, wou_ref, g_ref, wg_ref, wu_ref, wd_ref, o_ref):
    x1 = x_ref[...] + _dot(att_ref[...], woa_ref[...]) + _dot(u_ref[...], wou_ref[...])
    hb = _rms(x1, g_ref[...]).astype(BF16)
    gate = _dot(hb, wg_ref[...])
    act = (gate * _sigmoid(gate) * _dot(hb, wu_ref[...])).astype(BF16)
    o_ref[...] = x1 + _dot(act, wd_ref[...])


def _l0_out_ffn(x, att, u, woa, wou, g, wg, wu, wd):
    t = x.shape[0]
    tm = TM_FFN0
    row = lambda w: pl.BlockSpec((tm, w), lambda i: (i, 0))
    return pl.pallas_call(
        _l0_out_ffn_kernel,
        grid=(t // tm,),
        in_specs=[row(D_MODEL), row(FOX_WIDTH), row(CONV_CHANNELS), _const_spec(woa.shape),
                  _const_spec(wou.shape), _const_spec(g.shape), _const_spec(wg.shape),
                  _const_spec(wu.shape), _const_spec(wd.shape)],
        out_specs=row(D_MODEL),
        out_shape=jax.ShapeDtypeStruct((t, D_MODEL), F32),
        compiler_params=_cparams(("arbitrary",), 56),
        name="l0_out_ffn",
    )(x, att, u, woa, wou, g, wg, wu, wd)


def _l1_in_kernel(x_ref, g_ref, wuvo_ref, wfc_ref, bfc_ref, wgr_ref, bgr_ref, cw_ref, cb_ref,
                  wq_ref, wkt_ref, q_ref, kt_ref, v_ref, og_ref, lfc_ref, gr_ref, win_ref,
                  *, tiles_per_batch):
    i = pl.program_id(0)
    tm = x_ref.shape[0]
    w = MLSTM_WIDTH
    hb = _rms(x_ref[...], g_ref[...]).astype(BF16)
    uvo = _dot(hb, wuvo_ref[...])
    v_ref[...] = uvo[:, w:2 * w].astype(BF16)
    og_ref[...] = _sigmoid(uvo[:, 2 * w:]).astype(BF16)

    lfc_ref[...] = _log_sigmoid(_dot(hb, wfc_ref[...]) + bfc_ref[...])
    grow = _dot_nt(wgr_ref[...], hb) + bgr_ref[...]
    is_i = lax.broadcasted_iota(I32, grow.shape, 0) < MLSTM_HEADS
    gr_ref[...] = jnp.where(is_i, grow, _log_sigmoid(grow))

    @pl.when(i % tiles_per_batch == 0)
    def _():
        win_ref[0:L1_HALO, :] = jnp.zeros((L1_HALO, w), F32)

    win_ref[L1_HALO:L1_HALO + tm, :] = uvo[:, :w]
    first = L1_HALO - (MLSTM_CONV_WIDTH - 1)
    acc = jnp.broadcast_to(cb_ref[...], (tm, w))
    for j in range(MLSTM_CONV_WIDTH):
        acc = acc + win_ref[first + j:first + j + tm, :] * cw_ref[j:j + 1, :]
    win_ref[0:L1_HALO, :] = win_ref[tm:tm + L1_HALO, :]
    uc = (acc * _sigmoid(acc)).astype(BF16)
    d = MLSTM_HEAD_DIM
    for h in range(MLSTM_HEADS):
        uh = uc[:, h * d:(h + 1) * d]
        q_ref[:, h * d:(h + 1) * d] = _dot(uh, wq_ref[h]).astype(BF16)
        kt_ref[h * d:(h + 1) * d, :] = (_dot_nt(wkt_ref[h], uh) * (d ** -0.5)).astype(BF16)


def _l1_in(x, g, wuvo, wfc, bfc, wgr, bgr, cw, cb, wq, wkt, *, batch):
    t = x.shape[0]
    tm = TM_IN1
    tpb = (t // batch) // tm
    w = MLSTM_WIDTH
    row = lambda n: pl.BlockSpec((tm, n), lambda i: (i, 0))
    col = lambda n: pl.BlockSpec((n, tm), lambda i: (0, i))
    consts = [g, wuvo, wfc, bfc, wgr, bgr, cw, cb, wq, wkt]
    return pl.pallas_call(
        functools.partial(_l1_in_kernel, tiles_per_batch=tpb),
        grid=(t // tm,),
        in_specs=[row(D_MODEL)] + [_const_spec(a.shape) for a in consts],
        out_specs=[row(w), col(w), row(w), row(w), row(LANES), col(2 * MLSTM_HEADS)],
        out_shape=[jax.ShapeDtypeStruct((t, w), BF16), jax.ShapeDtypeStruct((w, t), BF16),
                   jax.ShapeDtypeStruct((t, w), BF16), jax.ShapeDtypeStruct((t, w), BF16),
                   jax.ShapeDtypeStruct((t, LANES), F32),
                   jax.ShapeDtypeStruct((2 * MLSTM_HEADS, t), F32)],
        scratch_shapes=[pltpu.VMEM((L1_HALO + tm, w), F32)],
        compiler_params=_cparams(("arbitrary",), 48),
        name="l1_in",
    )(x, *consts)


def _mlstm_kernel(q_ref, kt_ref, v_ref, og_ref, lfc_ref, gr_ref, hg_ref, o_ref, c_scr, m_scr):
    ci = pl.program_id(1)
    L = q_ref.shape[0]
    d = MLSTM_HEAD_DIM
    nh = MLSTM_HEADS

    @pl.when(ci == 0)
    def _():
        c_scr[...] = jnp.zeros_like(c_scr)
        m_scr[...] = jnp.zeros_like(m_scr)

    r = lax.broadcasted_iota(I32, (L, L), 0)
    c = lax.broadcasted_iota(I32, (L, L), 1)
    tril = r >= c
    tri_lo = jnp.where(tril, 1.0, 0.0).astype(BF16)
    tri_up = jnp.where(r <= c, 1.0, 0.0).astype(BF16)

    bc_col = sum(_dot(tri_lo, p) for p in _split3(lfc_ref[...]))
    bc_row = sum(_dot(p, tri_up) for p in _split3(gr_ref[nh:2 * nh, :]))
    ones_blk = jnp.ones((L, d), BF16)
    heads = range(nh)
    sls = [slice(h * d, (h + 1) * d) for h in heads]

    cts = [c_scr[h] for h in heads]
    v_augs = [jnp.concatenate([v_ref[:, sls[h]], ones_blk], axis=1) for h in heads]
    qk = [_dot(q_ref[:, sls[h]], kt_ref[sls[h], :]) for h in heads]
    qc = [_dot(q_ref[:, sls[h]], cts[h].astype(BF16)) for h in heads]

    w_intra, w_inter, m_ts, new_state = [], [], [], []
    for h in heads:
        bcol = bc_col[:, h:h + 1]
        brow = bc_row[h:h + 1, :]
        srow = gr_ref[h:h + 1, :] - brow
        g = brow[:, L - 1:L]
        m_old = m_scr[h][:, 0:1]
        dmat = jnp.where(tril, bcol + srow, NEG)
        inter = bcol + m_old
        m_t = jnp.maximum(inter, jnp.max(dmat, axis=1, keepdims=True))
        w_inter.append(jnp.exp(inter - m_t))
        w_intra.append((jnp.exp(dmat - m_t) * qk[h]).astype(BF16))
        m_ts.append(m_t)
        a_row = g + srow
        m_new = jnp.maximum(g + m_old, jnp.max(a_row, axis=1, keepdims=True))
        kw = (kt_ref[sls[h], :].astype(F32) * jnp.exp(a_row - m_new)).astype(BF16)
        new_state.append((jnp.exp(g + m_old - m_new), kw, m_new))

    for h in heads:
        res = w_inter[h] * qc[h] + _dot(w_intra[h], v_augs[h])
        den = jnp.maximum(jnp.abs(res[:, d:]), jnp.exp(-m_ts[h]))
        hc = og_ref[:, sls[h]].astype(F32) * (res[:, :d] / den)
        o_ref[:, sls[h]] = _rms(hc, hg_ref[:, sls[h]]).astype(o_ref.dtype)

    for h in heads:
        decay, kw, m_new = new_state[h]
        c_scr[h] = decay * cts[h] + _dot(kw, v_augs[h])
        m_scr[h] = jnp.broadcast_to(m_new, (1, LANES))


def _mlstm(q, kt, v, og, lfc, gr, hg, *, batch):
    t = q.shape[0]
    w = MLSTM_WIDTH
    nc = (t // batch) // CHUNK
    row = lambda n: pl.BlockSpec((CHUNK, n), lambda b, c: (b * nc + c, 0))
    col = lambda n: pl.BlockSpec((n, CHUNK), lambda b, c: (0, b * nc + c))
    return pl.pallas_call(
        _mlstm_kernel,
        grid=(batch, nc),
        in_specs=[row(w), col(w), row(w), row(w), row(LANES), col(2 * MLSTM_HEADS), _const_spec(hg.shape)],
        out_specs=row(w),
        out_shape=jax.ShapeDtypeStruct((t, w), BF16),
        scratch_shapes=[pltpu.VMEM((MLSTM_HEADS, MLSTM_HEAD_DIM, 2 * MLSTM_HEAD_DIM), F32),
                        pltpu.VMEM((MLSTM_HEADS, 1, LANES), F32)],
        compiler_params=_cparams(("arbitrary", "arbitrary"), 32),
        name="mlstm",
    )(q, kt, v, og, lfc, gr, hg)


def _l1_out_router_kernel(x_ref, hc_ref, wo_ref, g_ref, wr_ref, x3_ref, h3_ref, meta_ref, cnt_ref, carry_ref):
    i = pl.program_id(0)
    tm = x_ref.shape[0]

    @pl.when(i == 0)
    def _():
        carry_ref[...] = jnp.zeros_like(carry_ref)

    x3 = x_ref[...] + _dot(hc_ref[...], wo_ref[...])
    x3_ref[...] = x3
    h3 = _rms(x3, g_ref[...])
    for s in range(D_MODEL // LANES):
        h3_ref[pl.ds(s, tm, stride=D_MODEL // LANES), :] = h3[:, s * LANES:(s + 1) * LANES]

    h_hi, h_mid, _ = _split3(h3)
    w_hi, w_mid, _ = _split3(wr_ref[...])
    logits = _dot(h_hi, w_hi) + (_dot(h_hi, w_mid) + _dot(h_mid, w_hi))

    lane = lax.broadcasted_iota(I32, (tm, LANES), 1)
    lane_f = lane.astype(F32)
    lg = jnp.where(lane < N_EXPERTS, logits, NEG)
    v1 = jnp.max(lg, axis=1, keepdims=True)
    i1 = jnp.min(jnp.where(lg == v1, lane_f, float(LANES)), axis=1, keepdims=True)
    lg2 = jnp.where(lane_f == i1, NEG, lg)
    v2 = jnp.max(lg2, axis=1, keepdims=True)
    i2 = jnp.min(jnp.where(lg2 == v2, lane_f, float(LANES)), axis=1, keepdims=True)
    e = jnp.exp(v2 - v1)
    g1 = 1.0 / (1.0 + e)
    g2 = e / (1.0 + e)

    oh1 = lane_f == i1
    oh2 = lane_f == i2
    oh = jnp.where(oh1 | oh2, 1.0, 0.0)
    r = lax.broadcasted_iota(I32, (tm, tm), 0)
    c = lax.broadcasted_iota(I32, (tm, tm), 1)
    strict = jnp.where(c < r, 1.0, 0.0).astype(BF16)
    pos = _dot(strict, oh.astype(BF16)) + carry_ref[0:1, :]
    rank1 = jnp.sum(jnp.where(oh1, pos, 0.0), axis=1, keepdims=True)
    rank2 = jnp.sum(jnp.where(oh2, pos, 0.0), axis=1, keepdims=True)
    total = carry_ref[0:1, :] + jnp.sum(oh, axis=0, keepdims=True)
    carry_ref[...] = jnp.broadcast_to(total, carry_ref.shape)
    cnt_ref[...] = jnp.broadcast_to(total, cnt_ref.shape)

    meta = jnp.zeros((tm, LANES), F32)
    for k, val in enumerate((i1, i2, g1, g2, rank1, rank2)):
        meta = jnp.where(lane == k, val, meta)
    meta_ref[...] = meta


def _l1_out_router(x, hc, wo, g, wr):
    t = x.shape[0]
    tm = TM_OUT1
    groups = D_MODEL // LANES
    row = lambda n: pl.BlockSpec((tm, n), lambda i: (i, 0))
    return pl.pallas_call(
        _l1_out_router_kernel,
        grid=(t // tm,),
        in_specs=[row(D_MODEL), row(MLSTM_WIDTH), _const_spec(wo.shape), _const_spec(g.shape),
                  _const_spec(wr.shape)],
        out_specs=[row(D_MODEL), pl.BlockSpec((tm * groups, LANES), lambda i: (i, 0)), row(LANES),
                   pl.BlockSpec((SUBLANES, LANES), lambda i: (0, 0))],
        out_shape=[jax.ShapeDtypeStruct((t, D_MODEL), F32), jax.ShapeDtypeStruct((t * groups, LANES), F32),
                   jax.ShapeDtypeStruct((t, LANES), F32), jax.ShapeDtypeStruct((SUBLANES, LANES), F32)],
        scratch_shapes=[pltpu.VMEM((SUBLANES, LANES), F32)],
        compiler_params=_cparams(("arbitrary",), 32),
        name="l1_out_router",
    )(x, hc, wo, g, wr)


def _gather_rows(tab0_ref, tabn_ref, src_hbm, buf, sem, group=1):
    i = pl.program_id(0)
    n = pl.num_programs(0)
    rows = buf.shape[1] // group
    slot = i % 2

    def row_copy(s, item, r):
        src_row = item * group
        if group > 1 and not isinstance(item, int):
            src_row = pl.multiple_of(src_row, group)
        return pltpu.make_async_copy(src_hbm.at[pl.ds(src_row, group), :],
                                     buf.at[s, pl.ds(r * group, group), :], sem.at[s])

    @pl.when(i == 0)
    def _():
        def start(r, c):
            row_copy(0, tab0_ref[0, r], r).start()
            return c
        lax.fori_loop(0, rows, start, 0, unroll=8)

    @pl.when(i + 1 < n)
    def _():
        for r in range(rows):
            row_copy(1 - slot, tabn_ref[0, r], r).start()

    def wait(r, c):
        row_copy(slot, 0, r).wait()
        return c
    lax.fori_loop(0, rows, wait, 0, unroll=8)
    return slot


def _gather_specs(table, n):
    width = table.shape[2]

    def smem_tile(index_map):
        return pl.BlockSpec((None, 1, width), index_map, memory_space=pltpu.SMEM)

    return [smem_tile(lambda i: (0, 0, 0)), smem_tile(lambda i: (jnp.minimum(i + 1, n - 1), 0, 0))]


def _dispatch_kernel(tab0_ref, tabn_ref, h_hbm, o_ref, buf, sem):
    groups = D_MODEL // LANES
    rows = o_ref.shape[0]
    slot = _gather_rows(tab0_ref, tabn_ref, h_hbm, buf, sem, group=groups)
    for s in range(groups):
        o_ref[:, s * LANES:(s + 1) * LANES] = buf[slot, pl.ds(s, rows, stride=groups), :].astype(o_ref.dtype)


def _dispatch(src, h3):
    nt = src.shape[0]
    groups = D_MODEL // LANES
    return pl.pallas_call(
        _dispatch_kernel,
        grid=(nt,),
        in_specs=_gather_specs(src, nt) + [pl.BlockSpec(memory_space=pl.ANY)],
        out_specs=pl.BlockSpec((TM_DISPATCH, D_MODEL), lambda i: (i, 0)),
        out_shape=jax.ShapeDtypeStruct((nt * TM_DISPATCH, D_MODEL), BF16),
        scratch_shapes=[pltpu.VMEM((2, TM_DISPATCH * groups, LANES), F32), pltpu.SemaphoreType.DMA((2,))],
        compiler_params=_cparams(("arbitrary",), 48),
        name="dispatch",
    )(src, src, h3)


def _experts_kernel(te_ref, nused_ref, x_ref, wg_ref, wu_ref, wd_ref, o_ref):
    i = pl.program_id(0)
    j = pl.program_id(1)

    @pl.when(j == 0)
    def _():
        o_ref[...] = jnp.zeros_like(o_ref)

    @pl.when(i < nused_ref[0])
    def _():
        xb = x_ref[...]
        gate = _dot(xb, wg_ref[...])
        act = (gate * _sigmoid(gate) * _dot(xb, wu_ref[...])).astype(BF16)
        o_ref[...] += _dot(act, wd_ref[...])


def _experts(tile_expert, nused, xs, wg, wu, wd):
    nt = xs.shape[0] // TM_E
    nf = D_FF_EXPERT // TF_E

    def wcol(i, j, te, nu):
        return (te[i], 0, jnp.where(i < nu[0], j, nf - 1))

    def wrow(i, j, te, nu):
        return (te[i], jnp.where(i < nu[0], j, nf - 1), 0)

    grid_spec = pltpu.PrefetchScalarGridSpec(
        num_scalar_prefetch=2,
        grid=(nt, nf),
        in_specs=[pl.BlockSpec((TM_E, D_MODEL), lambda i, j, te, nu: (i, 0)),
                  pl.BlockSpec((None, D_MODEL, TF_E), wcol), pl.BlockSpec((None, D_MODEL, TF_E), wcol),
                  pl.BlockSpec((None, TF_E, D_MODEL), wrow)],
        out_specs=pl.BlockSpec((TM_E, D_MODEL), lambda i, j, te, nu: (i, 0)),
    )
    return pl.pallas_call(
        _experts_kernel,
        grid_spec=grid_spec,
        out_shape=jax.ShapeDtypeStruct((nt * TM_E, D_MODEL), F32),
        compiler_params=_cparams(("arbitrary", "arbitrary"), 48),
        name="experts",
    )(tile_expert, nused, xs, wg, wu, wd)


def _combine_kernel(pos0_ref, posn_ref, x_ref, meta_ref, g_ref, y_hbm, o_ref, y_scr, sem):
    tm = x_ref.shape[0]
    slot = _gather_rows(pos0_ref, posn_ref, y_hbm, y_scr, sem)
    meta = meta_ref[...]
    y = x_ref[...] + meta[:, 2:3] * y_scr[slot, 0:tm, :] + meta[:, 3:4] * y_scr[slot, tm:2 * tm, :]
    o_ref[...] = _rms(y, g_ref[...])


def _combine(pos, x3, y, meta, g):
    t = x3.shape[0]
    tm = TM_FIN
    nt = t // tm
    row = lambda n: pl.BlockSpec((tm, n), lambda i: (i, 0))
    return pl.pallas_call(
        _combine_kernel,
        grid=(nt,),
        in_specs=_gather_specs(pos, nt) + [row(D_MODEL), row(LANES), _const_spec(g.shape),
                                           pl.BlockSpec(memory_space=pl.ANY)],
        out_specs=row(D_MODEL),
        out_shape=jax.ShapeDtypeStruct((t, D_MODEL), F32),
        scratch_shapes=[pltpu.VMEM((2, 2 * tm, D_MODEL), F32), pltpu.SemaphoreType.DMA((2,))],
        compiler_params=_cparams(("arbitrary",), 56),
        name="combine_final_norm",
    )(pos, pos, x3, meta, g, y)


def _route(meta, cnt, t):
    idx = meta[:, 0:2].astype(I32)
    rank = meta[:, 4:6].astype(I32)
    counts = cnt[0, :N_EXPERTS].astype(I32)
    tiles_e = (counts + TM_E - 1) // TM_E
    tile_end = jnp.cumsum(tiles_e)
    row_off = (tile_end - tiles_e) * TM_E
    nt = 2 * t // TM_E + N_EXPERTS
    p = nt * TM_E
    pos = row_off[idx] + rank
    src = (jnp.arange(p, dtype=I32) % t).at[pos.reshape(-1)].set(jnp.repeat(jnp.arange(t, dtype=I32), 2),
                                                                 unique_indices=True)
    src = src.reshape(p // TM_DISPATCH, 1, TM_DISPATCH)
    tiles = jnp.arange(nt, dtype=I32)
    tile_expert = jnp.minimum(jnp.sum((tiles[:, None] >= tile_end[None, :]).astype(I32), axis=1),
                              N_EXPERTS - 1)
    nused = tile_end[-1:].astype(I32)
    pos_tiles = pos.reshape(t // TM_FIN, TM_FIN, 2).transpose(0, 2, 1).reshape(t // TM_FIN, 1, 2 * TM_FIN)
    return tile_expert, nused, src, pos_tiles


def kernel(x, l0_norm1_g, l0_w_in, l0_b_f, l0_dw_w, l0_dw_b, l0_cln_g, l0_cln_b, l0_w_out, l0_norm2_g, l0_w_gate, l0_w_up, l0_w_down, l1_norm1_g, l1_w_in, l1_b_i, l1_b_f, l1_conv_w, l1_conv_b, l1_wq_head, l1_wk_head, l1_hnorm_g, l1_w_out, l1_norm2_g, l1_router, l1_e_gate, l1_e_up, l1_e_down, final_norm_g):
    batch, seq, d = x.shape
    t = batch * seq
    xf = x.reshape(t, d)
    vec = lambda a: a.reshape(1, -1).astype(F32)
    bf = lambda a: a.astype(BF16)

    fw = FOX_WIDTH
    o_f = 3 * fw
    o_a = o_f + FOX_HEADS
    o_g = o_a + CONV_CHANNELS
    w_qv = jnp.concatenate([l0_w_in[:, :fw], l0_w_in[:, 2 * fw:o_f]], axis=1)
    q, kta, v, u = _l0_in(
        xf, vec(l0_norm1_g), bf(w_qv), bf(l0_w_in[:, fw:2 * fw].T), bf(l0_w_in[:, o_f:o_a].T),
        l0_b_f.reshape(-1, 1).astype(F32), bf(l0_w_in[:, o_a:o_g]), bf(l0_w_in[:, o_g:]), batch=batch)
    att = _fox_attention(q, kta, v, batch=batch)
    dw_w = jnp.concatenate([l0_dw_w, jnp.zeros((1, CONV_CHANNELS), F32)], axis=0)
    uc = _conv_module(u, dw_w, vec(l0_dw_b), vec(l0_cln_g), vec(l0_cln_b), batch=batch)
    x2 = _l0_out_ffn(xf, att, uc, bf(l0_w_out[:fw]), bf(l0_w_out[fw:]), vec(l0_norm2_g),
                     bf(l0_w_gate), bf(l0_w_up), bf(l0_w_down))

    w = MLSTM_WIDTH
    nh = MLSTM_HEADS
    w_i = l1_w_in[:, 3 * w:3 * w + nh]
    w_f = l1_w_in[:, 3 * w + nh:]
    wfc = bf(jnp.pad(w_f, ((0, 0), (0, LANES - nh))))
    bfc = jnp.pad(l1_b_f, (0, LANES - nh)).reshape(1, LANES).astype(F32)
    wgr = bf(jnp.concatenate([w_i.T, w_f.T], axis=0))
    bgr = jnp.concatenate([l1_b_i, l1_b_f]).reshape(-1, 1).astype(F32)
    ql, kt, vl, og, lfc, gr = _l1_in(
        x2, vec(l1_norm1_g), bf(l1_w_in[:, :3 * w]), wfc, bfc, wgr, bgr, l1_conv_w.astype(F32),
        vec(l1_conv_b), bf(l1_wq_head), bf(jnp.swapaxes(l1_wk_head, 1, 2)), batch=batch)
    hc = _mlstm(ql, kt, vl, og, lfc, gr, vec(l1_hnorm_g), batch=batch)
    wr = jnp.pad(l1_router.astype(F32), ((0, 0), (0, LANES - N_EXPERTS)))
    x3, h3, meta, cnt = _l1_out_router(x2, hc, bf(l1_w_out), vec(l1_norm2_g), wr)

    tile_expert, nused, src, pos_tiles = _route(meta, cnt, t)
    xs = _dispatch(src, h3)
    y = _experts(tile_expert, nused, xs, bf(l1_e_gate), bf(l1_e_up), bf(l1_e_down))
    out = _combine(pos_tiles, x3, y, meta, vec(final_norm_g))
    return out.reshape(batch, seq, d)
```

```python
import functools

import jax
import jax.numpy as jnp
from jax import lax
from jax.experimental import pallas as pl
from jax.experimental.pallas import tpu as pltpu

F32 = jnp.float32
BF16 = jnp.bfloat16
I32 = jnp.int32

EPS = 1e-6
NEG = -1e30
LOG2E = 1.4426950408889634

D_MODEL = 1024
FOX_HEADS = 8
FOX_HEAD_DIM = 64
FOX_WIDTH = FOX_HEADS * FOX_HEAD_DIM
CONV_CHANNELS = 512
CONV_WIDTH = 31
MLSTM_HEADS = 8
MLSTM_HEAD_DIM = 128
MLSTM_WIDTH = MLSTM_HEADS * MLSTM_HEAD_DIM
MLSTM_CONV_WIDTH = 4
D_FF = 2816
N_EXPERTS = 8
D_FF_EXPERT = 3584

LANES = 128
SUBLANES = 8
MIB = 1024 * 1024

TM_IN0 = 512
TQ = 512
FOX_Q_HALVES = 4
FOX_KAUG = 256
TM_CONV = 512
CONV_ROWS = 64
CONV_HALO = 32
TM_FFN0 = 512
TM_IN1 = 512
L1_HALO = 8
CHUNK = 256
TM_OUT1 = 512
TM_E = 512
TF_E = 512
GATHER_ROWS_PER_STEP = 80
EXPERT_SRC_WIDTH = 640
TM_FIN = 512


def _cparams(sem, vmem_mib):
    return pltpu.CompilerParams(dimension_semantics=sem, vmem_limit_bytes=vmem_mib * MIB)


def _rms(x, g):
    return x * lax.rsqrt(jnp.mean(x * x, axis=-1, keepdims=True) + EPS) * g


def _sigmoid(x):
    return 1.0 / (1.0 + jnp.exp(-x))


def _log_sigmoid(x):
    return jnp.minimum(x, 0.0) - jnp.log(1.0 + jnp.exp(-jnp.abs(x)))


def _dot(a, b):
    return jnp.dot(a, b, preferred_element_type=F32)


def _dot_nt(a, b):
    return lax.dot_general(a, b, (((1,), (1,)), ((), ())), preferred_element_type=F32)


def _split3(x):
    hi = x.astype(BF16)
    r1 = x - hi.astype(F32)
    mid = r1.astype(BF16)
    lo = (r1 - mid.astype(F32)).astype(BF16)
    return hi, mid, lo


def _const_spec(shape):
    nd = len(shape)
    return pl.BlockSpec(shape, lambda *_: (0,) * nd, pipeline_mode=pl.Buffered(1))


def _l0_in_kernel(x_ref, g_ref, wqv_ref, wkt_ref, wft_ref, bf_ref, wa_ref, wg_ref,
                  q_ref, kta_ref, v_ref, u_ref, carry_ref, *, tiles_per_batch):
    i = pl.program_id(0)
    tm = x_ref.shape[0]
    hb = _rms(x_ref[...], g_ref[...]).astype(BF16)
    qv = _dot(hb, wqv_ref[...])
    q_ref[...] = (qv[:, :FOX_WIDTH] * (FOX_HEAD_DIM ** -0.5 * LOG2E)).astype(BF16)
    v_ref[...] = qv[:, FOX_WIDTH:].astype(BF16)
    kt = _dot_nt(wkt_ref[...], hb).astype(BF16)
    u_ref[...] = _dot(hb, wa_ref[...]) * _sigmoid(_dot(hb, wg_ref[...]))

    logf = _log_sigmoid(_dot_nt(wft_ref[...], hb) + bf_ref[...])
    lane = lax.broadcasted_iota(I32, logf.shape, 1)
    cum = logf
    shift = 1
    while shift < tm:
        cum = cum + jnp.where(lane >= shift, pltpu.roll(cum, shift, 1), 0.0)
        shift *= 2

    @pl.when(i % tiles_per_batch == 0)
    def _():
        carry_ref[...] = jnp.zeros_like(carry_ref)

    c = cum + carry_ref[:, 0:1]
    carry_ref[...] = jnp.broadcast_to(c[:, tm - 1:tm], carry_ref.shape)

    pieces = [p.astype(F32) for p in _split3(c * LOG2E)]
    sub = lax.broadcasted_iota(I32, (2 * SUBLANES, tm), 0)
    zeros_tail = jnp.zeros((FOX_KAUG - LANES - 2 * SUBLANES, tm), BF16)
    for p in range(FOX_HEADS // 2):
        ext = jnp.zeros((2 * SUBLANES, tm), F32)
        for hh in range(2):
            for n, piece in enumerate(pieces):
                ext = jnp.where(sub == 3 * hh + n, piece[2 * p + hh:2 * p + hh + 1, :], ext)
        kta_ref[p, 0:LANES, :] = kt[p * LANES:(p + 1) * LANES, :]
        kta_ref[p, LANES:LANES + 2 * SUBLANES, :] = ext.astype(BF16)
        kta_ref[p, LANES + 2 * SUBLANES:, :] = zeros_tail


def _l0_in(x, g, wqv, wkt, wft, bf, wa, wg, *, batch):
    t = x.shape[0]
    tm = TM_IN0
    tpb = (t // batch) // tm
    pairs = FOX_HEADS // 2
    row = lambda w: pl.BlockSpec((tm, w), lambda i: (i, 0))
    consts = [g, wqv, wkt, wft, bf, wa, wg]
    return pl.pallas_call(
        functools.partial(_l0_in_kernel, tiles_per_batch=tpb),
        grid=(t // tm,),
        in_specs=[row(D_MODEL)] + [_const_spec(a.shape) for a in consts],
        out_specs=[row(FOX_WIDTH), pl.BlockSpec((pairs, FOX_KAUG, tm), lambda i: (0, 0, i)),
                   row(FOX_WIDTH), row(CONV_CHANNELS)],
        out_shape=[jax.ShapeDtypeStruct((t, FOX_WIDTH), BF16),
                   jax.ShapeDtypeStruct((pairs, FOX_KAUG, t), BF16),
                   jax.ShapeDtypeStruct((t, FOX_WIDTH), BF16),
                   jax.ShapeDtypeStruct((t, CONV_CHANNELS), F32)],
        scratch_shapes=[pltpu.VMEM((FOX_HEADS, LANES), F32)],
        compiler_params=_cparams(("arbitrary",), 40),
        name="l0_in",
    )(x, *consts)


def _fox_kernel(q_ref, kta_ref, v_ref, o_ref, q_scr, m_scr, acc_scr):
    qi = pl.program_id(2)
    tq = TQ
    tk = TQ
    lane = lax.broadcasted_iota(I32, (tq, LANES), 1)
    is_a = lane < FOX_HEAD_DIM
    lane_k = lax.broadcasted_iota(I32, (tk, LANES), 1)

    ext_a = jnp.where(lane < 3, -1.0, 0.0)
    ext_b = jnp.where((lane >= 3) & (lane < 6), -1.0, 0.0)
    for half in range(FOX_Q_HALVES):
        q2 = q_ref[half * tq:(half + 1) * tq, :].astype(F32)
        q_scr[2 * half] = jnp.concatenate([jnp.where(is_a, q2, 0.0), ext_a], axis=1).astype(BF16)
        q_scr[2 * half + 1] = jnp.concatenate([jnp.where(is_a, 0.0, q2), ext_b], axis=1).astype(BF16)
    m_scr[...] = jnp.full(m_scr.shape, NEG, F32)
    acc_scr[...] = jnp.zeros_like(acc_scr)

    def tile(ki, plan):
        start = pl.multiple_of(ki * tk, tk)
        kt = kta_ref[:, pl.ds(start, tk)]
        vf = v_ref[pl.ds(start, tk), :].astype(F32)
        v_augs = (jnp.where(lane_k < FOX_HEAD_DIM, vf, 1.0).astype(BF16),
                  jnp.where(lane_k < FOX_HEAD_DIM, 1.0, vf).astype(BF16))
        causal = (lax.broadcasted_iota(I32, (tq, tk), 1) <= lax.broadcasted_iota(I32, (tq, tk), 0))
        chains = [(2 * half + h, h, masked) for half, masked in plan for h in range(2)]
        logits = [_dot(q_scr[n], kt) for n, _, _ in chains]
        probs, alphas = [], []
        for (n, _, masked), s in zip(chains, logits):
            if masked:
                s = jnp.where(causal, s, NEG)
            m_old = m_scr[n]
            m_new = jnp.maximum(m_old, jnp.max(s, axis=1, keepdims=True))
            probs.append(jnp.concatenate(
                [jnp.exp2(s[:, j * LANES:(j + 1) * LANES] - m_new).astype(BF16) for j in range(tk // LANES)],
                axis=1))
            alphas.append(jnp.exp2(m_old - m_new))
            m_scr[n] = m_new
        for (n, h, _), p, alpha in zip(chains, probs, alphas):
            acc_scr[n] = acc_scr[n] * alpha + _dot(p, v_augs[h])

    def body(ki, carry):
        tile(ki, tuple((half, False) for half in range(FOX_Q_HALVES)))
        return carry

    first_diag = FOX_Q_HALVES * qi
    lax.fori_loop(0, first_diag, body, 0)
    for d in range(FOX_Q_HALVES):
        tile(first_diag + d, ((d, True),) + tuple((half, False) for half in range(d + 1, FOX_Q_HALVES)))

    for half in range(FOX_Q_HALVES):
        acc_a = acc_scr[2 * half]
        acc_b = acc_scr[2 * half + 1]
        o = jnp.where(is_a, acc_a / pltpu.roll(acc_a, FOX_HEAD_DIM, 1),
                      acc_b / pltpu.roll(acc_b, FOX_HEAD_DIM, 1))
        o_ref[half * tq:(half + 1) * tq, :] = o.astype(o_ref.dtype)


def _fox_attention(q, kta, v, *, batch):
    t = q.shape[0]
    seq = t // batch
    rows = FOX_Q_HALVES * TQ
    nq = seq // rows
    pairs = FOX_HEADS // 2
    nstate = 2 * FOX_Q_HALVES
    return pl.pallas_call(
        _fox_kernel,
        grid=(batch, pairs, nq),
        in_specs=[pl.BlockSpec((rows, LANES), lambda b, p, i: (b * nq + i, p)),
                  pl.BlockSpec((None, FOX_KAUG, seq), lambda b, p, i: (p, 0, b)),
                  pl.BlockSpec((seq, LANES), lambda b, p, i: (b, p))],
        out_specs=pl.BlockSpec((rows, LANES), lambda b, p, i: (b * nq + i, p)),
        out_shape=jax.ShapeDtypeStruct((t, FOX_WIDTH), BF16),
        scratch_shapes=[pltpu.VMEM((nstate, TQ, FOX_KAUG), BF16),
                        pltpu.VMEM((nstate, TQ, LANES), F32), pltpu.VMEM((nstate, TQ, LANES), F32)],
        compiler_params=_cparams(("arbitrary", "arbitrary", "arbitrary"), 40),
        name="fox_attention",
    )(q, kta, v)


def _conv_mod_kernel(u_ref, w_ref, b_ref, g_ref, beta_ref, o_ref, win_ref, sh_ref, *, tiles_per_batch):
    i = pl.program_id(0)
    tm = u_ref.shape[0]

    @pl.when(i % tiles_per_batch == 0)
    def _():
        win_ref[0:CONV_HALO, :] = jnp.zeros((CONV_HALO, CONV_CHANNELS), F32)

    win_ref[CONV_HALO:CONV_HALO + tm, :] = u_ref[...]
    n_sh = sh_ref.shape[1]
    for b in range(1, SUBLANES):
        sh_ref[b - 1] = win_ref[b:b + n_sh, :]
    first = CONV_HALO - (CONV_WIDTH - 1)
    for c in range(tm // CONV_ROWS):
        r0 = c * CONV_ROWS
        acc = jnp.broadcast_to(b_ref[...], (CONV_ROWS, CONV_CHANNELS))
        for j in range(CONV_WIDTH):
            b = (first + j) % SUBLANES
            a = r0 + first + j - b
            tap = win_ref[a:a + CONV_ROWS, :] if b == 0 else sh_ref[b - 1, a:a + CONV_ROWS, :]
            acc = acc + tap * w_ref[j:j + 1, :]
        mu = jnp.mean(acc, axis=-1, keepdims=True)
        d = acc - mu
        var = jnp.mean(d * d, axis=-1, keepdims=True)
        y = d * lax.rsqrt(var + EPS) * g_ref[...] + beta_ref[...]
        o_ref[r0:r0 + CONV_ROWS, :] = (y * _sigmoid(y)).astype(o_ref.dtype)
    win_ref[0:CONV_HALO, :] = win_ref[tm:tm + CONV_HALO, :]


def _conv_module(u, w, b, g, beta, *, batch):
    t = u.shape[0]
    tm = TM_CONV
    tpb = (t // batch) // tm
    return pl.pallas_call(
        functools.partial(_conv_mod_kernel, tiles_per_batch=tpb),
        grid=(t // tm,),
        in_specs=[pl.BlockSpec((tm, CONV_CHANNELS), lambda i: (i, 0)), _const_spec(w.shape),
                  _const_spec(b.shape), _const_spec(g.shape), _const_spec(beta.shape)],
        out_specs=pl.BlockSpec((tm, CONV_CHANNELS), lambda i: (i, 0)),
        out_shape=jax.ShapeDtypeStruct((t, CONV_CHANNELS), BF16),
        scratch_shapes=[pltpu.VMEM((CONV_HALO + tm, CONV_CHANNELS), F32),
                        pltpu.VMEM((SUBLANES - 1, CONV_HALO + tm - SUBLANES, CONV_CHANNELS), F32)],
        compiler_params=_cparams(("arbitrary",), 32),
        name="conv_module",
    )(u, w, b, g, beta)


def _l0_out_ffn_kernel(x_ref, att_ref, u_ref, woa_ref, wou_ref, g_ref, wg_ref, wu_ref, wd_ref, o_ref):
    x1 = x_ref[...] + _dot(att_ref[...], woa_ref[...]) + _dot(u_ref[...], wou_ref[...])
    hb = _rms(x1, g_ref[...]).astype(BF16)
    gate = _dot(hb, wg_ref[...])
    act = (gate * _sigmoid(gate) * _dot(hb, wu_ref[...])).astype(BF16)
    o_ref[...] = x1 + _dot(act, wd_ref[...])


def _l0_out_ffn(x, att, u, woa, wou, g, wg, wu, wd):
    t = x.shape[0]
    tm = TM_FFN0
    row = lambda w: pl.BlockSpec((tm, w), lambda i: (i, 0))
    return pl.pallas_call(
        _l0_out_ffn_kernel,
        grid=(t // tm,),
        in_specs=[row(D_MODEL), row(FOX_WIDTH), row(CONV_CHANNELS), _const_spec(woa.shape),
                  _const_spec(wou.shape), _const_spec(g.shape), _const_spec(wg.shape),
                  _const_spec(wu.shape), _const_spec(wd.shape)],
        out_specs=row(D_MODEL),
        out_shape=jax.ShapeDtypeStruct((t, D_MODEL), F32),
        compiler_params=_cparams(("arbitrary",), 56),
        name="l0_out_ffn",
    )(x, att, u, woa, wou, g, wg, wu, wd)


def _l1_in_kernel(x_ref, g_ref, wuvo_ref, wfc_ref, bfc_ref, wgr_ref, bgr_ref, cw_ref, cb_ref,
                  wq_ref, wkt_ref, q_ref, kt_ref, v_ref, og_ref, lfc_ref, gr_ref, win_ref,
                  *, tiles_per_batch):
    i = pl.program_id(0)
    tm = x_ref.shape[0]
    w = MLSTM_WIDTH
    hb = _rms(x_ref[...], g_ref[...]).astype(BF16)
    uvo = _dot(hb, wuvo_ref[...])
    v_ref[...] = uvo[:, w:2 * w].astype(BF16)
    og_ref[...] = _sigmoid(uvo[:, 2 * w:]).astype(BF16)

    lfc_ref[...] = _log_sigmoid(_dot(hb, wfc_ref[...]) + bfc_ref[...])
    grow = _dot_nt(wgr_ref[...], hb) + bgr_ref[...]
    is_i = lax.broadcasted_iota(I32, grow.shape, 0) < MLSTM_HEADS
    gr_ref[...] = jnp.where(is_i, grow, _log_sigmoid(grow))

    @pl.when(i % tiles_per_batch == 0)
    def _():
        win_ref[0:L1_HALO, :] = jnp.zeros((L1_HALO, w), F32)

    win_ref[L1_HALO:L1_HALO + tm, :] = uvo[:, :w]
    first = L1_HALO - (MLSTM_CONV_WIDTH - 1)
    acc = jnp.broadcast_to(cb_ref[...], (tm, w))
    for j in range(MLSTM_CONV_WIDTH):
        acc = acc + win_ref[first + j:first + j + tm, :] * cw_ref[j:j + 1, :]
    win_ref[0:L1_HALO, :] = win_ref[tm:tm + L1_HALO, :]
    uc = (acc * _sigmoid(acc)).astype(BF16)
    d = MLSTM_HEAD_DIM
    for h in range(MLSTM_HEADS):
        uh = uc[:, h * d:(h + 1) * d]
        q_ref[:, h * d:(h + 1) * d] = _dot(uh, wq_ref[h]).astype(BF16)
        kt_ref[h * d:(h + 1) * d, :] = (_dot_nt(wkt_ref[h], uh) * (d ** -0.5)).astype(BF16)


def _l1_in(x, g, wuvo, wfc, bfc, wgr, bgr, cw, cb, wq, wkt, *, batch):
    t = x.shape[0]
    tm = TM_IN1
    tpb = (t // batch) // tm
    w = MLSTM_WIDTH
    row = lambda n: pl.BlockSpec((tm, n), lambda i: (i, 0))
    col = lambda n: pl.BlockSpec((n, tm), lambda i: (0, i))
    consts = [g, wuvo, wfc, bfc, wgr, bgr, cw, cb, wq, wkt]
    return pl.pallas_call(
        functools.partial(_l1_in_kernel, tiles_per_batch=tpb),
        grid=(t // tm,),
        in_specs=[row(D_MODEL)] + [_const_spec(a.shape) for a in consts],
        out_specs=[row(w), col(w), row(w), row(w), row(LANES), col(2 * MLSTM_HEADS)],
        out_shape=[jax.ShapeDtypeStruct((t, w), BF16), jax.ShapeDtypeStruct((w, t), BF16),
                   jax.ShapeDtypeStruct((t, w), BF16), jax.ShapeDtypeStruct((t, w), BF16),
                   jax.ShapeDtypeStruct((t, LANES), F32),
                   jax.ShapeDtypeStruct((2 * MLSTM_HEADS, t), F32)],
        scratch_shapes=[pltpu.VMEM((L1_HALO + tm, w), F32)],
        compiler_params=_cparams(("arbitrary",), 48),
        name="l1_in",
    )(x, *consts)


def _mlstm_kernel(q_ref, kt_ref, v_ref, og_ref, lfc_ref, gr_ref, hg_ref, o_ref, c_scr, m_scr):
    ci = pl.program_id(1)
    L = q_ref.shape[0]
    d = MLSTM_HEAD_DIM
    nh = MLSTM_HEADS

    @pl.when(ci == 0)
    def _():
        c_scr[...] = jnp.zeros_like(c_scr)
        m_scr[...] = jnp.zeros_like(m_scr)

    r = lax.broadcasted_iota(I32, (L, L), 0)
    c = lax.broadcasted_iota(I32, (L, L), 1)
    tril = r >= c
    tri_lo = jnp.where(tril, 1.0, 0.0).astype(BF16)
    tri_up = jnp.where(r <= c, 1.0, 0.0).astype(BF16)

    bc_col = sum(_dot(tri_lo, p) for p in _split3(lfc_ref[...]))
    bc_row = sum(_dot(p, tri_up) for p in _split3(gr_ref[nh:2 * nh, :]))
    ones_blk = jnp.ones((L, d), BF16)
    heads = range(nh)
    sls = [slice(h * d, (h + 1) * d) for h in heads]

    cts = [c_scr[h] for h in heads]
    v_augs = [jnp.concatenate([v_ref[:, sls[h]], ones_blk], axis=1) for h in heads]
    qk = [_dot(q_ref[:, sls[h]], kt_ref[sls[h], :]) for h in heads]
    qc = [_dot(q_ref[:, sls[h]], cts[h].astype(BF16)) for h in heads]

    w_intra, w_inter, m_ts, new_state = [], [], [], []
    for h in heads:
        bcol = bc_col[:, h:h + 1]
        brow = bc_row[h:h + 1, :]
        srow = gr_ref[h:h + 1, :] - brow
        g = brow[:, L - 1:L]
        m_old = m_scr[h][:, 0:1]
        dmat = jnp.where(tril, bcol + srow, NEG)
        inter = bcol + m_old
        m_t = jnp.maximum(inter, jnp.max(dmat, axis=1, keepdims=True))
        w_inter.append(jnp.exp(inter - m_t))
        w_intra.append((jnp.exp(dmat - m_t) * qk[h]).astype(BF16))
        m_ts.append(m_t)
        a_row = g + srow
        m_new = jnp.maximum(g + m_old, jnp.max(a_row, axis=1, keepdims=True))
        kw = (kt_ref[sls[h], :].astype(F32) * jnp.exp(a_row - m_new)).astype(BF16)
        new_state.append((jnp.exp(g + m_old - m_new), kw, m_new))

    for h in heads:
        res = w_inter[h] * qc[h] + _dot(w_intra[h], v_augs[h])
        den = jnp.maximum(jnp.abs(res[:, d:]), jnp.exp(-m_ts[h]))
        hc = og_ref[:, sls[h]].astype(F32) * (res[:, :d] / den)
        o_ref[:, sls[h]] = _rms(hc, hg_ref[:, sls[h]]).astype(o_ref.dtype)

    for h in heads:
        decay, kw, m_new = new_state[h]
        c_scr[h] = decay * cts[h] + _dot(kw, v_augs[h])
        m_scr[h] = jnp.broadcast_to(m_new, (1, LANES))


def _mlstm(q, kt, v, og, lfc, gr, hg, *, batch):
    t = q.shape[0]
    w = MLSTM_WIDTH
    nc = (t // batch) // CHUNK
    row = lambda n: pl.BlockSpec((CHUNK, n), lambda b, c: (b * nc + c, 0))
    col = lambda n: pl.BlockSpec((n, CHUNK), lambda b, c: (0, b * nc + c))
    return pl.pallas_call(
        _mlstm_kernel,
        grid=(batch, nc),
        in_specs=[row(w), col(w), row(w), row(w), row(LANES), col(2 * MLSTM_HEADS), _const_spec(hg.shape)],
        out_specs=row(w),
        out_shape=jax.ShapeDtypeStruct((t, w), BF16),
        scratch_shapes=[pltpu.VMEM((MLSTM_HEADS, MLSTM_HEAD_DIM, 2 * MLSTM_HEAD_DIM), F32),
                        pltpu.VMEM((MLSTM_HEADS, 1, LANES), F32)],
        compiler_params=_cparams(("arbitrary", "arbitrary"), 32),
        name="mlstm",
    )(q, kt, v, og, lfc, gr, hg)


def _l1_out_router_kernel(x_ref, hc_ref, wo_ref, g_ref, wr_ref, x3_ref, h3_ref, meta_ref, cnt_ref, carry_ref):
    i = pl.program_id(0)
    tm = x_ref.shape[0]

    @pl.when(i == 0)
    def _():
        carry_ref[...] = jnp.zeros_like(carry_ref)

    x3 = x_ref[...] + _dot(hc_ref[...], wo_ref[...])
    x3_ref[...] = x3
    h3 = _rms(x3, g_ref[...])
    for s in range(D_MODEL // LANES):
        h3_ref[pl.ds(s, tm, stride=D_MODEL // LANES), :] = h3[:, s * LANES:(s + 1) * LANES]

    h_hi, h_mid, _ = _split3(h3)
    w_hi, w_mid, _ = _split3(wr_ref[...])
    logits = _dot(h_hi, w_hi) + (_dot(h_hi, w_mid) + _dot(h_mid, w_hi))

    lane = lax.broadcasted_iota(I32, (tm, LANES), 1)
    lane_f = lane.astype(F32)
    lg = jnp.where(lane < N_EXPERTS, logits, NEG)
    v1 = jnp.max(lg, axis=1, keepdims=True)
    i1 = jnp.min(jnp.where(lg == v1, lane_f, float(LANES)), axis=1, keepdims=True)
    lg2 = jnp.where(lane_f == i1, NEG, lg)
    v2 = jnp.max(lg2, axis=1, keepdims=True)
    i2 = jnp.min(jnp.where(lg2 == v2, lane_f, float(LANES)), axis=1, keepdims=True)
    e = jnp.exp(v2 - v1)
    g1 = 1.0 / (1.0 + e)
    g2 = e / (1.0 + e)

    oh1 = lane_f == i1
    oh2 = lane_f == i2
    oh = jnp.where(oh1 | oh2, 1.0, 0.0)
    r = lax.broadcasted_iota(I32, (tm, tm), 0)
    c = lax.broadcasted_iota(I32, (tm, tm), 1)
    strict = jnp.where(c < r, 1.0, 0.0).astype(BF16)
    pos = _dot(strict, oh.astype(BF16)) + carry_ref[0:1, :]
    rank1 = jnp.sum(jnp.where(oh1, pos, 0.0), axis=1, keepdims=True)
    rank2 = jnp.sum(jnp.where(oh2, pos, 0.0), axis=1, keepdims=True)
    total = carry_ref[0:1, :] + jnp.sum(oh, axis=0, keepdims=True)
    carry_ref[...] = jnp.broadcast_to(total, carry_ref.shape)
    cnt_ref[...] = jnp.broadcast_to(total, cnt_ref.shape)

    meta = jnp.zeros((tm, LANES), F32)
    for k, val in enumerate((i1, i2, g1, g2, rank1, rank2)):
        meta = jnp.where(lane == k, val, meta)
    meta_ref[...] = meta


def _l1_out_router(x, hc, wo, g, wr):
    t = x.shape[0]
    tm = TM_OUT1
    groups = D_MODEL // LANES
    row = lambda n: pl.BlockSpec((tm, n), lambda i: (i, 0))
    return pl.pallas_call(
        _l1_out_router_kernel,
        grid=(t // tm,),
        in_specs=[row(D_MODEL), row(MLSTM_WIDTH), _const_spec(wo.shape), _const_spec(g.shape),
                  _const_spec(wr.shape)],
        out_specs=[row(D_MODEL), pl.BlockSpec((tm * groups, LANES), lambda i: (i, 0)), row(LANES),
                   pl.BlockSpec((SUBLANES, LANES), lambda i: (0, 0))],
        out_shape=[jax.ShapeDtypeStruct((t, D_MODEL), F32), jax.ShapeDtypeStruct((t * groups, LANES), F32),
                   jax.ShapeDtypeStruct((t, LANES), F32), jax.ShapeDtypeStruct((SUBLANES, LANES), F32)],
        scratch_shapes=[pltpu.VMEM((SUBLANES, LANES), F32)],
        compiler_params=_cparams(("arbitrary",), 32),
        name="l1_out_router",
    )(x, hc, wo, g, wr)


def _gather_rows(tab0_ref, tabn_ref, src_hbm, buf, sem):
    i = pl.program_id(0)
    n = pl.num_programs(0)
    rows = buf.shape[1]
    slot = i % 2

    def row_copy(s, src_row, r):
        return pltpu.make_async_copy(src_hbm.at[pl.ds(src_row, 1), :], buf.at[s, pl.ds(r, 1), :], sem.at[s])

    @pl.when(i == 0)
    def _():
        def start(r, c):
            row_copy(0, tab0_ref[0, r], r).start()
            return c
        lax.fori_loop(0, rows, start, 0, unroll=8)

    @pl.when(i + 1 < n)
    def _():
        for r in range(rows):
            row_copy(1 - slot, tabn_ref[0, r], r).start()

    def wait(r, c):
        row_copy(slot, 0, r).wait()
        return c
    lax.fori_loop(0, rows, wait, 0, unroll=8)
    return slot


def _gather_specs(table, n):
    width = table.shape[2]

    def smem_tile(index_map):
        return pl.BlockSpec((None, 1, width), index_map, memory_space=pltpu.SMEM)

    return [smem_tile(lambda i: (0, 0, 0)), smem_tile(lambda i: (jnp.minimum(i + 1, n - 1), 0, 0))]


def _experts_gather_kernel(te_ref, nused_ref, src0_ref, srcn_ref, h_hbm, wg_ref, wu_ref, wd_ref, o_ref,
                           x_scr, xb_scr, gsem):
    i = pl.program_id(0)
    j = pl.program_id(1)
    nt = pl.num_programs(0)
    nf = pl.num_programs(1)
    tm = o_ref.shape[0]
    groups = D_MODEL // LANES
    rows = x_scr.shape[1] // groups
    slot = i % 2

    def copy(s, tok, r):
        src_row = tok * groups
        dst_row = r * groups
        if not isinstance(tok, int):
            src_row = pl.multiple_of(src_row, groups)
        if not isinstance(r, int):
            dst_row = pl.multiple_of(dst_row, groups)
        return pltpu.make_async_copy(h_hbm.at[pl.ds(src_row, groups), :],
                                     x_scr.at[s, pl.ds(dst_row, groups), :], gsem.at[s])

    def wait_tile(s):
        def wait(r, c):
            copy(s, 0, r).wait()
            return c
        lax.fori_loop(0, rows, wait, 0, unroll=8)

    def prefetch_next():
        for k in range(GATHER_ROWS_PER_STEP):
            r = j * GATHER_ROWS_PER_STEP + k
            copy(1 - slot, srcn_ref[0, r], r).start()

    @pl.when((i == 0) & (j == 0))
    def _():
        def start(r, c):
            copy(0, src0_ref[0, r], r).start()
            return c
        lax.fori_loop(0, rows, start, 0, unroll=8)

    @pl.when(j == 0)
    def _():
        wait_tile(slot)
        for s in range(groups):
            xb_scr[:, s * LANES:(s + 1) * LANES] = x_scr[slot, pl.ds(s, tm, stride=groups), :].astype(BF16)
        o_ref[...] = jnp.zeros_like(o_ref)

    @pl.when(i < nused_ref[0])
    def _():
        xb = xb_scr[...]
        gate = _dot(xb, wg_ref[...])
        up = _dot(xb, wu_ref[...])
        prefetch_next()
        act = (gate * _sigmoid(gate) * up).astype(BF16)
        o_ref[...] += _dot(act, wd_ref[...])

    @pl.when(i >= nused_ref[0])
    def _():
        prefetch_next()

    @pl.when((i == nt - 1) & (j == nf - 1))
    def _():
        wait_tile(1 - slot)


def _experts_gather(tile_expert, nused, src, h3, wg, wu, wd):
    nt = src.shape[0] - 1
    nf = D_FF_EXPERT // TF_E
    groups = D_MODEL // LANES
    rows = nf * GATHER_ROWS_PER_STEP

    def wcol(i, j, te, nu):
        return (te[i], 0, jnp.where(i < nu[0], j, nf - 1))

    def wrow(i, j, te, nu):
        return (te[i], jnp.where(i < nu[0], j, nf - 1), 0)

    def smem_tile(index_map):
        return pl.BlockSpec((None, 1, src.shape[2]), index_map, memory_space=pltpu.SMEM)

    grid_spec = pltpu.PrefetchScalarGridSpec(
        num_scalar_prefetch=2,
        grid=(nt, nf),
        in_specs=[smem_tile(lambda i, j, te, nu: (0, 0, 0)), smem_tile(lambda i, j, te, nu: (i + 1, 0, 0)),
                  pl.BlockSpec(memory_space=pl.ANY),
                  pl.BlockSpec((None, D_MODEL, TF_E), wcol), pl.BlockSpec((None, D_MODEL, TF_E), wcol),
                  pl.BlockSpec((None, TF_E, D_MODEL), wrow)],
        out_specs=pl.BlockSpec((TM_E, D_MODEL), lambda i, j, te, nu: (i, 0)),
        scratch_shapes=[pltpu.VMEM((2, rows * groups, LANES), F32), pltpu.VMEM((TM_E, D_MODEL), BF16),
                        pltpu.SemaphoreType.DMA((2,))],
    )
    return pl.pallas_call(
        _experts_gather_kernel,
        grid_spec=grid_spec,
        out_shape=jax.ShapeDtypeStruct((nt * TM_E, D_MODEL), F32),
        compiler_params=_cparams(("arbitrary", "arbitrary"), 48),
        name="experts",
    )(tile_expert, nused, src, src, h3, wg, wu, wd)


def _combine_kernel(pos0_ref, posn_ref, x_ref, meta_ref, g_ref, y_hbm, o_ref, y_scr, sem):
    tm = x_ref.shape[0]
    slot = _gather_rows(pos0_ref, posn_ref, y_hbm, y_scr, sem)
    meta = meta_ref[...]
    y = x_ref[...] + meta[:, 2:3] * y_scr[slot, 0:tm, :] + meta[:, 3:4] * y_scr[slot, tm:2 * tm, :]
    o_ref[...] = _rms(y, g_ref[...])


def _combine(pos, x3, y, meta, g):
    t = x3.shape[0]
    tm = TM_FIN
    nt = t // tm
    row = lambda n: pl.BlockSpec((tm, n), lambda i: (i, 0))
    return pl.pallas_call(
        _combine_kernel,
        grid=(nt,),
        in_specs=_gather_specs(pos, nt) + [row(D_MODEL), row(LANES), _const_spec(g.shape),
                                           pl.BlockSpec(memory_space=pl.ANY)],
        out_specs=row(D_MODEL),
        out_shape=jax.ShapeDtypeStruct((t, D_MODEL), F32),
        scratch_shapes=[pltpu.VMEM((2, 2 * tm, D_MODEL), F32), pltpu.SemaphoreType.DMA((2,))],
        compiler_params=_cparams(("arbitrary",), 56),
        name="combine_final_norm",
    )(pos, pos, x3, meta, g, y)


def _route(meta, cnt, t):
    idx = meta[:, 0:2].astype(I32)
    rank = meta[:, 4:6].astype(I32)
    counts = cnt[0, :N_EXPERTS].astype(I32)
    tiles_e = (counts + TM_E - 1) // TM_E
    tile_end = jnp.cumsum(tiles_e)
    row_off = (tile_end - tiles_e) * TM_E
    nt = 2 * t // TM_E + N_EXPERTS
    p = nt * TM_E
    pos = row_off[idx] + rank
    src = (jnp.arange(p, dtype=I32) % t).at[pos.reshape(-1)].set(jnp.repeat(jnp.arange(t, dtype=I32), 2),
                                                                 unique_indices=True)
    src = src.reshape(nt, TM_E)
    extra_cols = (src[:, :EXPERT_SRC_WIDTH - TM_E] + t // 2) % t
    src = jnp.concatenate([src, extra_cols], axis=1)
    src = jnp.concatenate([src, (jnp.arange(EXPERT_SRC_WIDTH, dtype=I32) % t)[None, :]], axis=0)
    src = src.reshape(nt + 1, 1, EXPERT_SRC_WIDTH)
    tiles = jnp.arange(nt, dtype=I32)
    tile_expert = jnp.minimum(jnp.sum((tiles[:, None] >= tile_end[None, :]).astype(I32), axis=1),
                              N_EXPERTS - 1)
    nused = tile_end[-1:].astype(I32)
    pos_tiles = pos.reshape(t // TM_FIN, TM_FIN, 2).transpose(0, 2, 1).reshape(t // TM_FIN, 1, 2 * TM_FIN)
    return tile_expert, nused, src, pos_tiles


def kernel(x, l0_norm1_g, l0_w_in, l0_b_f, l0_dw_w, l0_dw_b, l0_cln_g, l0_cln_b, l0_w_out, l0_norm2_g, l0_w_gate, l0_w_up, l0_w_down, l1_norm1_g, l1_w_in, l1_b_i, l1_b_f, l1_conv_w, l1_conv_b, l1_wq_head, l1_wk_head, l1_hnorm_g, l1_w_out, l1_norm2_g, l1_router, l1_e_gate, l1_e_up, l1_e_down, final_norm_g):
    batch, seq, d = x.shape
    t = batch * seq
    xf = x.reshape(t, d)
    vec = lambda a: a.reshape(1, -1).astype(F32)
    bf = lambda a: a.astype(BF16)

    fw = FOX_WIDTH
    o_f = 3 * fw
    o_a = o_f + FOX_HEADS
    o_g = o_a + CONV_CHANNELS
    w_qv = jnp.concatenate([l0_w_in[:, :fw], l0_w_in[:, 2 * fw:o_f]], axis=1)
    q, kta, v, u = _l0_in(
        xf, vec(l0_norm1_g), bf(w_qv), bf(l0_w_in[:, fw:2 * fw].T), bf(l0_w_in[:, o_f:o_a].T),
        l0_b_f.reshape(-1, 1).astype(F32), bf(l0_w_in[:, o_a:o_g]), bf(l0_w_in[:, o_g:]), batch=batch)
    att = _fox_attention(q, kta, v, batch=batch)
    dw_w = jnp.concatenate([l0_dw_w, jnp.zeros((1, CONV_CHANNELS), F32)], axis=0)
    uc = _conv_module(u, dw_w, vec(l0_dw_b), vec(l0_cln_g), vec(l0_cln_b), batch=batch)
    x2 = _l0_out_ffn(xf, att, uc, bf(l0_w_out[:fw]), bf(l0_w_out[fw:]), vec(l0_norm2_g),
                     bf(l0_w_gate), bf(l0_w_up), bf(l0_w_down))

    w = MLSTM_WIDTH
    nh = MLSTM_HEADS
    w_i = l1_w_in[:, 3 * w:3 * w + nh]
    w_f = l1_w_in[:, 3 * w + nh:]
    wfc = bf(jnp.pad(w_f, ((0, 0), (0, LANES - nh))))
    bfc = jnp.pad(l1_b_f, (0, LANES - nh)).reshape(1, LANES).astype(F32)
    wgr = bf(jnp.concatenate([w_i.T, w_f.T], axis=0))
    bgr = jnp.concatenate([l1_b_i, l1_b_f]).reshape(-1, 1).astype(F32)
    ql, kt, vl, og, lfc, gr = _l1_in(
        x2, vec(l1_norm1_g), bf(l1_w_in[:, :3 * w]), wfc, bfc, wgr, bgr, l1_conv_w.astype(F32),
        vec(l1_conv_b), bf(l1_wq_head), bf(jnp.swapaxes(l1_wk_head, 1, 2)), batch=batch)
    hc = _mlstm(ql, kt, vl, og, lfc, gr, vec(l1_hnorm_g), batch=batch)
    wr = jnp.pad(l1_router.astype(F32), ((0, 0), (0, LANES - N_EXPERTS)))
    x3, h3, meta, cnt = _l1_out_router(x2, hc, bf(l1_w_out), vec(l1_norm2_g), wr)

    tile_expert, nused, src, pos_tiles = _route(meta, cnt, t)
    y = _experts_gather(tile_expert, nused, src, h3, bf(l1_e_gate), bf(l1_e_up), bf(l1_e_down))
    out = _combine(pos_tiles, x3, y, meta, vec(final_norm_g))
    return out.reshape(batch, seq, d)
```

```python
import functools

import jax
import jax.numpy as jnp
from jax import lax
from jax.experimental import pallas as pl
from jax.experimental.pallas import tpu as pltpu

F32 = jnp.float32
BF16 = jnp.bfloat16
I32 = jnp.int32

EPS = 1e-6
NEG = -1e30
LOG2E = 1.4426950408889634

D_MODEL = 1024
FOX_HEADS = 8
FOX_HEAD_DIM = 64
FOX_WIDTH = FOX_HEADS * FOX_HEAD_DIM
CONV_CHANNELS = 512
CONV_WIDTH = 31
MLSTM_HEADS = 8
MLSTM_HEAD_DIM = 128
MLSTM_WIDTH = MLSTM_HEADS * MLSTM_HEAD_DIM
MLSTM_CONV_WIDTH = 4
D_FF = 2816
N_EXPERTS = 8
D_FF_EXPERT = 3584

LANES = 128
SUBLANES = 8
MIB = 1024 * 1024

TM_IN0 = 512
TQ = 512
FOX_Q_HALVES = 4
FOX_KAUG = 256
CONV_ROWS = 64
CONV_HALO = 32
TM_FFN0 = 512
TM_IN1 = 512
L1_HALO = 8
CHUNK = 256
TM_OUT1 = 512
TM_E = 512
TF_E = 512
GATHER_ROWS_PER_STEP = 80
EXPERT_SRC_WIDTH = 640
TM_FIN = 512


def _cparams(sem, vmem_mib):
    return pltpu.CompilerParams(dimension_semantics=sem, vmem_limit_bytes=vmem_mib * MIB)


def _rms(x, g):
    return x * lax.rsqrt(jnp.mean(x * x, axis=-1, keepdims=True) + EPS) * g


def _sigmoid(x):
    return 1.0 / (1.0 + jnp.exp(-x))


def _log_sigmoid(x):
    return jnp.minimum(x, 0.0) - jnp.log(1.0 + jnp.exp(-jnp.abs(x)))


def _dot(a, b):
    return jnp.dot(a, b, preferred_element_type=F32)


def _dot_nt(a, b):
    return lax.dot_general(a, b, (((1,), (1,)), ((), ())), preferred_element_type=F32)


def _split3(x):
    hi = x.astype(BF16)
    r1 = x - hi.astype(F32)
    mid = r1.astype(BF16)
    lo = (r1 - mid.astype(F32)).astype(BF16)
    return hi, mid, lo


def _const_spec(shape):
    nd = len(shape)
    return pl.BlockSpec(shape, lambda *_: (0,) * nd, pipeline_mode=pl.Buffered(1))


def _l0_in_kernel(x_ref, g_ref, wqv_ref, wkt_ref, wft_ref, bf_ref, wa_ref, wg_ref,
                  cw_ref, cb_ref, cg_ref, cbeta_ref,
                  q_ref, kta_ref, v_ref, uc_ref, carry_ref, win_ref, sh_ref, *, tiles_per_batch):
    i = pl.program_id(0)
    tm = x_ref.shape[0]

    @pl.when(i % tiles_per_batch == 0)
    def _():
        carry_ref[...] = jnp.zeros_like(carry_ref)
        win_ref[0:CONV_HALO, :] = jnp.zeros((CONV_HALO, CONV_CHANNELS), F32)

    hb = _rms(x_ref[...], g_ref[...]).astype(BF16)
    win_ref[CONV_HALO:CONV_HALO + tm, :] = _dot(hb, wa_ref[...]) * _sigmoid(_dot(hb, wg_ref[...]))
    _conv_ln_swish(win_ref, sh_ref, cw_ref, cb_ref, cg_ref, cbeta_ref, uc_ref)

    qv = _dot(hb, wqv_ref[...])
    q_ref[...] = (qv[:, :FOX_WIDTH] * (FOX_HEAD_DIM ** -0.5 * LOG2E)).astype(BF16)
    v_ref[...] = qv[:, FOX_WIDTH:].astype(BF16)
    kt = _dot_nt(wkt_ref[...], hb).astype(BF16)

    logf = _log_sigmoid(_dot_nt(wft_ref[...], hb) + bf_ref[...])
    lane = lax.broadcasted_iota(I32, logf.shape, 1)
    cum = logf
    shift = 1
    while shift < tm:
        cum = cum + jnp.where(lane >= shift, pltpu.roll(cum, shift, 1), 0.0)
        shift *= 2

    c = cum + carry_ref[:, 0:1]
    carry_ref[...] = jnp.broadcast_to(c[:, tm - 1:tm], carry_ref.shape)

    pieces = [p.astype(F32) for p in _split3(c * LOG2E)]
    sub = lax.broadcasted_iota(I32, (2 * SUBLANES, tm), 0)
    zeros_tail = jnp.zeros((FOX_KAUG - LANES - 2 * SUBLANES, tm), BF16)
    for p in range(FOX_HEADS // 2):
        ext = jnp.zeros((2 * SUBLANES, tm), F32)
        for hh in range(2):
            for n, piece in enumerate(pieces):
                ext = jnp.where(sub == 3 * hh + n, piece[2 * p + hh:2 * p + hh + 1, :], ext)
        kta_ref[p, 0:LANES, :] = kt[p * LANES:(p + 1) * LANES, :]
        kta_ref[p, LANES:LANES + 2 * SUBLANES, :] = ext.astype(BF16)
        kta_ref[p, LANES + 2 * SUBLANES:, :] = zeros_tail


def _l0_in(x, g, wqv, wkt, wft, bf, wa, wg, cw, cb, cg, cbeta, *, batch):
    t = x.shape[0]
    tm = TM_IN0
    tpb = (t // batch) // tm
    pairs = FOX_HEADS // 2
    row = lambda w: pl.BlockSpec((tm, w), lambda i: (i, 0))
    consts = [g, wqv, wkt, wft, bf, wa, wg, cw, cb, cg, cbeta]
    return pl.pallas_call(
        functools.partial(_l0_in_kernel, tiles_per_batch=tpb),
        grid=(t // tm,),
        in_specs=[row(D_MODEL)] + [_const_spec(a.shape) for a in consts],
        out_specs=[row(FOX_WIDTH), pl.BlockSpec((pairs, FOX_KAUG, tm), lambda i: (0, 0, i)),
                   row(FOX_WIDTH), row(CONV_CHANNELS)],
        out_shape=[jax.ShapeDtypeStruct((t, FOX_WIDTH), BF16),
                   jax.ShapeDtypeStruct((pairs, FOX_KAUG, t), BF16),
                   jax.ShapeDtypeStruct((t, FOX_WIDTH), BF16),
                   jax.ShapeDtypeStruct((t, CONV_CHANNELS), BF16)],
        scratch_shapes=[pltpu.VMEM((FOX_HEADS, LANES), F32),
                        pltpu.VMEM((CONV_HALO + tm, CONV_CHANNELS), F32),
                        pltpu.VMEM((SUBLANES - 1, CONV_HALO + tm - SUBLANES, CONV_CHANNELS), F32)],
        compiler_params=_cparams(("arbitrary",), 48),
        name="l0_in",
    )(x, *consts)


def _fox_kernel(q_ref, kta_ref, v_ref, o_ref, q_scr, m_scr, acc_scr):
    qi = pl.program_id(2)
    tq = TQ
    tk = TQ
    lane = lax.broadcasted_iota(I32, (tq, LANES), 1)
    is_a = lane < FOX_HEAD_DIM
    lane_k = lax.broadcasted_iota(I32, (tk, LANES), 1)

    ext_a = jnp.where(lane < 3, -1.0, 0.0)
    ext_b = jnp.where((lane >= 3) & (lane < 6), -1.0, 0.0)
    for half in range(FOX_Q_HALVES):
        q2 = q_ref[half * tq:(half + 1) * tq, :].astype(F32)
        q_scr[2 * half] = jnp.concatenate([jnp.where(is_a, q2, 0.0), ext_a], axis=1).astype(BF16)
        q_scr[2 * half + 1] = jnp.concatenate([jnp.where(is_a, 0.0, q2), ext_b], axis=1).astype(BF16)
    m_scr[...] = jnp.full(m_scr.shape, NEG, F32)
    acc_scr[...] = jnp.zeros_like(acc_scr)

    def tile(ki, plan):
        start = pl.multiple_of(ki * tk, tk)
        kt = kta_ref[:, pl.ds(start, tk)]
        vf = v_ref[pl.ds(start, tk), :].astype(F32)
        v_augs = (jnp.where(lane_k < FOX_HEAD_DIM, vf, 1.0).astype(BF16),
                  jnp.where(lane_k < FOX_HEAD_DIM, 1.0, vf).astype(BF16))
        causal = (lax.broadcasted_iota(I32, (tq, tk), 1) <= lax.broadcasted_iota(I32, (tq, tk), 0))
        chains = [(2 * half + h, h, masked) for half, masked in plan for h in range(2)]
        logits = [_dot(q_scr[n], kt) for n, _, _ in chains]
        probs, alphas = [], []
        for (n, _, masked), s in zip(chains, logits):
            if masked:
                s = jnp.where(causal, s, NEG)
            m_old = m_scr[n]
            m_new = jnp.maximum(m_old, jnp.max(s, axis=1, keepdims=True))
            probs.append(jnp.concatenate(
                [jnp.exp2(s[:, j * LANES:(j + 1) * LANES] - m_new).astype(BF16) for j in range(tk // LANES)],
                axis=1))
            alphas.append(jnp.exp2(m_old - m_new))
            m_scr[n] = m_new
        for (n, h, _), p, alpha in zip(chains, probs, alphas):
            acc_scr[n] = acc_scr[n] * alpha + _dot(p, v_augs[h])

    def body(ki, carry):
        tile(ki, tuple((half, False) for half in range(FOX_Q_HALVES)))
        return carry

    first_diag = FOX_Q_HALVES * qi
    lax.fori_loop(0, first_diag, body, 0)
    for d in range(FOX_Q_HALVES):
        tile(first_diag + d, ((d, True),) + tuple((half, False) for half in range(d + 1, FOX_Q_HALVES)))

    for half in range(FOX_Q_HALVES):
        acc_a = acc_scr[2 * half]
        acc_b = acc_scr[2 * half + 1]
        o = jnp.where(is_a, acc_a / pltpu.roll(acc_a, FOX_HEAD_DIM, 1),
                      acc_b / pltpu.roll(acc_b, FOX_HEAD_DIM, 1))
        o_ref[half * tq:(half + 1) * tq, :] = o.astype(o_ref.dtype)


def _fox_attention(q, kta, v, *, batch):
    t = q.shape[0]
    seq = t // batch
    rows = FOX_Q_HALVES * TQ
    nq = seq // rows
    pairs = FOX_HEADS // 2
    nstate = 2 * FOX_Q_HALVES
    return pl.pallas_call(
        _fox_kernel,
        grid=(batch, pairs, nq),
        in_specs=[pl.BlockSpec((rows, LANES), lambda b, p, i: (b * nq + i, p)),
                  pl.BlockSpec((None, FOX_KAUG, seq), lambda b, p, i: (p, 0, b)),
                  pl.BlockSpec((seq, LANES), lambda b, p, i: (b, p))],
        out_specs=pl.BlockSpec((rows, LANES), lambda b, p, i: (b * nq + i, p)),
        out_shape=jax.ShapeDtypeStruct((t, FOX_WIDTH), BF16),
        scratch_shapes=[pltpu.VMEM((nstate, TQ, FOX_KAUG), BF16),
                        pltpu.VMEM((nstate, TQ, LANES), F32), pltpu.VMEM((nstate, TQ, LANES), F32)],
        compiler_params=_cparams(("arbitrary", "arbitrary", "arbitrary"), 40),
        name="fox_attention",
    )(q, kta, v)


def _conv_ln_swish(win_ref, sh_ref, w_ref, b_ref, g_ref, beta_ref, o_ref):
    tm = o_ref.shape[0]
    n_sh = sh_ref.shape[1]
    for b in range(1, SUBLANES):
        sh_ref[b - 1] = win_ref[b:b + n_sh, :]
    first = CONV_HALO - (CONV_WIDTH - 1)
    for c in range(tm // CONV_ROWS):
        r0 = c * CONV_ROWS
        acc = jnp.broadcast_to(b_ref[...], (CONV_ROWS, CONV_CHANNELS))
        for j in range(CONV_WIDTH):
            b = (first + j) % SUBLANES
            a = r0 + first + j - b
            tap = win_ref[a:a + CONV_ROWS, :] if b == 0 else sh_ref[b - 1, a:a + CONV_ROWS, :]
            acc = acc + tap * w_ref[j:j + 1, :]
        mu = jnp.mean(acc, axis=-1, keepdims=True)
        d = acc - mu
        var = jnp.mean(d * d, axis=-1, keepdims=True)
        y = d * lax.rsqrt(var + EPS) * g_ref[...] + beta_ref[...]
        o_ref[r0:r0 + CONV_ROWS, :] = (y * _sigmoid(y)).astype(o_ref.dtype)
    win_ref[0:CONV_HALO, :] = win_ref[tm:tm + CONV_HALO, :]


def _l0_out_ffn_kernel(x_ref, att_ref, u_ref, woa_ref, wou_ref, g_ref, wg_ref, wu_ref, wd_ref, o_ref):
    x1 = x_ref[...] + _dot(att_ref[...], woa_ref[...]) + _dot(u_ref[...], wou_ref[...])
    hb = _rms(x1, g_ref[...]).astype(BF16)
    gate = _dot(hb, wg_ref[...])
    act = (gate * _sigmoid(gate) * _dot(hb, wu_ref[...])).astype(BF16)
    o_ref[...] = x1 + _dot(act, wd_ref[...])


def _l0_out_ffn(x, att, u, woa, wou, g, wg, wu, wd):
    t = x.shape[0]
    tm = TM_FFN0
    row = lambda w: pl.BlockSpec((tm, w), lambda i: (i, 0))
    return pl.pallas_call(
        _l0_out_ffn_kernel,
        grid=(t // tm,),
        in_specs=[row(D_MODEL), row(FOX_WIDTH), row(CONV_CHANNELS), _const_spec(woa.shape),
                  _const_spec(wou.shape), _const_spec(g.shape), _const_spec(wg.shape),
                  _const_spec(wu.shape), _const_spec(wd.shape)],
        out_specs=row(D_MODEL),
        out_shape=jax.ShapeDtypeStruct((t, D_MODEL), F32),
        compiler_params=_cparams(("arbitrary",), 56),
        name="l0_out_ffn",
    )(x, att, u, woa, wou, g, wg, wu, wd)


def _l1_in_kernel(x_ref, g_ref, wuvo_ref, wfc_ref, bfc_ref, wgr_ref, bgr_ref, cw_ref, cb_ref,
                  wq_ref, wkt_ref, q_ref, kt_ref, v_ref, og_ref, lfc_ref, gr_ref, win_ref,
                  *, tiles_per_batch):
    i = pl.program_id(0)
    tm = x_ref.shape[0]
    w = MLSTM_WIDTH
    hb = _rms(x_ref[...], g_ref[...]).astype(BF16)
    uvo = _dot(hb, wuvo_ref[...])
    v_ref[...] = uvo[:, w:2 * w].astype(BF16)
    og_ref[...] = _sigmoid(uvo[:, 2 * w:]).astype(BF16)

    lfc_ref[...] = _log_sigmoid(_dot(hb, wfc_ref[...]) + bfc_ref[...])
    grow = _dot_nt(wgr_ref[...], hb) + bgr_ref[...]
    is_i = lax.broadcasted_iota(I32, grow.shape, 0) < MLSTM_HEADS
    gr_ref[...] = jnp.where(is_i, grow, _log_sigmoid(grow))

    @pl.when(i % tiles_per_batch == 0)
    def _():
        win_ref[0:L1_HALO, :] = jnp.zeros((L1_HALO, w), F32)

    win_ref[L1_HALO:L1_HALO + tm, :] = uvo[:, :w]
    first = L1_HALO - (MLSTM_CONV_WIDTH - 1)
    acc = jnp.broadcast_to(cb_ref[...], (tm, w))
    for j in range(MLSTM_CONV_WIDTH):
        acc = acc + win_ref[first + j:first + j + tm, :] * cw_ref[j:j + 1, :]
    win_ref[0:L1_HALO, :] = win_ref[tm:tm + L1_HALO, :]
    uc = (acc * _sigmoid(acc)).astype(BF16)
    d = MLSTM_HEAD_DIM
    for h in range(MLSTM_HEADS):
        uh = uc[:, h * d:(h + 1) * d]
        q_ref[:, h * d:(h + 1) * d] = _dot(uh, wq_ref[h]).astype(BF16)
        kt_ref[h * d:(h + 1) * d, :] = (_dot_nt(wkt_ref[h], uh) * (d ** -0.5)).astype(BF16)


def _l1_in(x, g, wuvo, wfc, bfc, wgr, bgr, cw, cb, wq, wkt, *, batch):
    t = x.shape[0]
    tm = TM_IN1
    tpb = (t // batch) // tm
    w = MLSTM_WIDTH
    row = lambda n: pl.BlockSpec((tm, n), lambda i: (i, 0))
    col = lambda n: pl.BlockSpec((n, tm), lambda i: (0, i))
    consts = [g, wuvo, wfc, bfc, wgr, bgr, cw, cb, wq, wkt]
    return pl.pallas_call(
        functools.partial(_l1_in_kernel, tiles_per_batch=tpb),
        grid=(t // tm,),
        in_specs=[row(D_MODEL)] + [_const_spec(a.shape) for a in consts],
        out_specs=[row(w), col(w), row(w), row(w), row(LANES), col(2 * MLSTM_HEADS)],
        out_shape=[jax.ShapeDtypeStruct((t, w), BF16), jax.ShapeDtypeStruct((w, t), BF16),
                   jax.ShapeDtypeStruct((t, w), BF16), jax.ShapeDtypeStruct((t, w), BF16),
                   jax.ShapeDtypeStruct((t, LANES), F32),
                   jax.ShapeDtypeStruct((2 * MLSTM_HEADS, t), F32)],
        scratch_shapes=[pltpu.VMEM((L1_HALO + tm, w), F32)],
        compiler_params=_cparams(("arbitrary",), 48),
        name="l1_in",
    )(x, *consts)


def _mlstm_kernel(q_ref, kt_ref, v_ref, og_ref, lfc_ref, gr_ref, hg_ref, o_ref, c_scr, m_scr):
    ci = pl.program_id(1)
    L = q_ref.shape[0]
    d = MLSTM_HEAD_DIM
    nh = MLSTM_HEADS

    @pl.when(ci == 0)
    def _():
        c_scr[...] = jnp.zeros_like(c_scr)
        m_scr[...] = jnp.zeros_like(m_scr)

    r = lax.broadcasted_iota(I32, (L, L), 0)
    c = lax.broadcasted_iota(I32, (L, L), 1)
    tril = r >= c
    tri_lo = jnp.where(tril, 1.0, 0.0).astype(BF16)
    tri_up = jnp.where(r <= c, 1.0, 0.0).astype(BF16)

    bc_col = sum(_dot(tri_lo, p) for p in _split3(lfc_ref[...]))
    bc_row = sum(_dot(p, tri_up) for p in _split3(gr_ref[nh:2 * nh, :]))
    ones_blk = jnp.ones((L, d), BF16)
    heads = range(nh)
    sls = [slice(h * d, (h + 1) * d) for h in heads]

    cts = [c_scr[h] for h in heads]
    v_augs = [jnp.concatenate([v_ref[:, sls[h]], ones_blk], axis=1) for h in heads]
    qk = [_dot(q_ref[:, sls[h]], kt_ref[sls[h], :]) for h in heads]
    qc = [_dot(q_ref[:, sls[h]], cts[h].astype(BF16)) for h in heads]

    w_intra, w_inter, m_ts, new_state = [], [], [], []
    for h in heads:
        bcol = bc_col[:, h:h + 1]
        brow = bc_row[h:h + 1, :]
        srow = gr_ref[h:h + 1, :] - brow
        g = brow[:, L - 1:L]
        m_old = m_scr[h][:, 0:1]
        dmat = jnp.where(tril, bcol + srow, NEG)
        inter = bcol + m_old
        m_t = jnp.maximum(inter, jnp.max(dmat, axis=1, keepdims=True))
        w_inter.append(jnp.exp(inter - m_t))
        w_intra.append((jnp.exp(dmat - m_t) * qk[h]).astype(BF16))
        m_ts.append(m_t)
        a_row = g + srow
        m_new = jnp.maximum(g + m_old, jnp.max(a_row, axis=1, keepdims=True))
        kw = (kt_ref[sls[h], :].astype(F32) * jnp.exp(a_row - m_new)).astype(BF16)
        new_state.append((jnp.exp(g + m_old - m_new), kw, m_new))

    for h in heads:
        res = w_inter[h] * qc[h] + _dot(w_intra[h], v_augs[h])
        den = jnp.maximum(jnp.abs(res[:, d:]), jnp.exp(-m_ts[h]))
        hc = og_ref[:, sls[h]].astype(F32) * (res[:, :d] / den)
        o_ref[:, sls[h]] = _rms(hc, hg_ref[:, sls[h]]).astype(o_ref.dtype)

    for h in heads:
        decay, kw, m_new = new_state[h]
        c_scr[h] = decay * cts[h] + _dot(kw, v_augs[h])
        m_scr[h] = jnp.broadcast_to(m_new, (1, LANES))


def _mlstm(q, kt, v, og, lfc, gr, hg, *, batch):
    t = q.shape[0]
    w = MLSTM_WIDTH
    nc = (t // batch) // CHUNK
    row = lambda n: pl.BlockSpec((CHUNK, n), lambda b, c: (b * nc + c, 0))
    col = lambda n: pl.BlockSpec((n, CHUNK), lambda b, c: (0, b * nc + c))
    return pl.pallas_call(
        _mlstm_kernel,
        grid=(batch, nc),
        in_specs=[row(w), col(w), row(w), row(w), row(LANES), col(2 * MLSTM_HEADS), _const_spec(hg.shape)],
        out_specs=row(w),
        out_shape=jax.ShapeDtypeStruct((t, w), BF16),
        scratch_shapes=[pltpu.VMEM((MLSTM_HEADS, MLSTM_HEAD_DIM, 2 * MLSTM_HEAD_DIM), F32),
                        pltpu.VMEM((MLSTM_HEADS, 1, LANES), F32)],
        compiler_params=_cparams(("arbitrary", "arbitrary"), 32),
        name="mlstm",
    )(q, kt, v, og, lfc, gr, hg)


def _l1_out_router_kernel(x_ref, hc_ref, wo_ref, g_ref, wr_ref, x3_ref, h3_ref, meta_ref, cnt_ref, carry_ref):
    i = pl.program_id(0)
    tm = x_ref.shape[0]

    @pl.when(i == 0)
    def _():
        carry_ref[...] = jnp.zeros_like(carry_ref)

    x3 = x_ref[...] + _dot(hc_ref[...], wo_ref[...])
    x3_ref[...] = x3
    h3 = _rms(x3, g_ref[...])
    for s in range(D_MODEL // LANES):
        h3_ref[pl.ds(s, tm, stride=D_MODEL // LANES), :] = h3[:, s * LANES:(s + 1) * LANES]

    h_hi, h_mid, _ = _split3(h3)
    w_hi, w_mid, _ = _split3(wr_ref[...])
    logits = _dot(h_hi, w_hi) + (_dot(h_hi, w_mid) + _dot(h_mid, w_hi))

    lane = lax.broadcasted_iota(I32, (tm, LANES), 1)
    lane_f = lane.astype(F32)
    lg = jnp.where(lane < N_EXPERTS, logits, NEG)
    v1 = jnp.max(lg, axis=1, keepdims=True)
    i1 = jnp.min(jnp.where(lg == v1, lane_f, float(LANES)), axis=1, keepdims=True)
    lg2 = jnp.where(lane_f == i1, NEG, lg)
    v2 = jnp.max(lg2, axis=1, keepdims=True)
    i2 = jnp.min(jnp.where(lg2 == v2, lane_f, float(LANES)), axis=1, keepdims=True)
    e = jnp.exp(v2 - v1)
    g1 = 1.0 / (1.0 + e)
    g2 = e / (1.0 + e)

    oh1 = lane_f == i1
    oh2 = lane_f == i2
    oh = jnp.where(oh1 | oh2, 1.0, 0.0)
    r = lax.broadcasted_iota(I32, (tm, tm), 0)
    c = lax.broadcasted_iota(I32, (tm, tm), 1)
    strict = jnp.where(c < r, 1.0, 0.0).astype(BF16)
    pos = _dot(strict, oh.astype(BF16)) + carry_ref[0:1, :]
    rank1 = jnp.sum(jnp.where(oh1, pos, 0.0), axis=1, keepdims=True)
    rank2 = jnp.sum(jnp.where(oh2, pos, 0.0), axis=1, keepdims=True)
    total = carry_ref[0:1, :] + jnp.sum(oh, axis=0, keepdims=True)
    carry_ref[...] = jnp.broadcast_to(total, carry_ref.shape)
    cnt_ref[...] = jnp.broadcast_to(total, cnt_ref.shape)

    meta = jnp.zeros((tm, LANES), F32)
    for k, val in enumerate((i1, i2, g1, g2, rank1, rank2)):
        meta = jnp.where(lane == k, val, meta)
    meta_ref[...] = meta


def _l1_out_router(x, hc, wo, g, wr):
    t = x.shape[0]
    tm = TM_OUT1
    groups = D_MODEL // LANES
    row = lambda n: pl.BlockSpec((tm, n), lambda i: (i, 0))
    return pl.pallas_call(
        _l1_out_router_kernel,
        grid=(t // tm,),
        in_specs=[row(D_MODEL), row(MLSTM_WIDTH), _const_spec(wo.shape), _const_spec(g.shape),
                  _const_spec(wr.shape)],
        out_specs=[row(D_MODEL), pl.BlockSpec((tm * groups, LANES), lambda i: (i, 0)), row(LANES),
                   pl.BlockSpec((SUBLANES, LANES), lambda i: (0, 0))],
        out_shape=[jax.ShapeDtypeStruct((t, D_MODEL), F32), jax.ShapeDtypeStruct((t * groups, LANES), F32),
                   jax.ShapeDtypeStruct((t, LANES), F32), jax.ShapeDtypeStruct((SUBLANES, LANES), F32)],
        scratch_shapes=[pltpu.VMEM((SUBLANES, LANES), F32)],
        compiler_params=_cparams(("arbitrary",), 32),
        name="l1_out_router",
    )(x, hc, wo, g, wr)


def _gather_rows(tab0_ref, tabn_ref, src_hbm, buf, sem):
    i = pl.program_id(0)
    n = pl.num_programs(0)
    rows = buf.shape[1]
    slot = i % 2

    def row_copy(s, src_row, r):
        return pltpu.make_async_copy(src_hbm.at[pl.ds(src_row, 1), :], buf.at[s, pl.ds(r, 1), :], sem.at[s])

    @pl.when(i == 0)
    def _():
        def start(r, c):
            row_copy(0, tab0_ref[0, r], r).start()
            return c
        lax.fori_loop(0, rows, start, 0, unroll=8)

    @pl.when(i + 1 < n)
    def _():
        for r in range(rows):
            row_copy(1 - slot, tabn_ref[0, r], r).start()

    def wait(r, c):
        row_copy(slot, 0, r).wait()
        return c
    lax.fori_loop(0, rows, wait, 0, unroll=8)
    return slot


def _gather_specs(table, n):
    width = table.shape[2]

    def smem_tile(index_map):
        return pl.BlockSpec((None, 1, width), index_map, memory_space=pltpu.SMEM)

    return [smem_tile(lambda i: (0, 0, 0)), smem_tile(lambda i: (jnp.minimum(i + 1, n - 1), 0, 0))]


def _experts_gather_kernel(te_ref, nused_ref, src0_ref, srcn_ref, h_hbm, wg_ref, wu_ref, wd_ref, o_ref,
                           x_scr, xb_scr, gsem):
    i = pl.program_id(0)
    j = pl.program_id(1)
    nt = pl.num_programs(0)
    nf = pl.num_programs(1)
    tm = o_ref.shape[0]
    groups = D_MODEL // LANES
    rows = x_scr.shape[1] // groups
    slot = i % 2

    def copy(s, tok, r):
        src_row = tok * groups
        dst_row = r * groups
        if not isinstance(tok, int):
            src_row = pl.multiple_of(src_row, groups)
        if not isinstance(r, int):
            dst_row = pl.multiple_of(dst_row, groups)
        return pltpu.make_async_copy(h_hbm.at[pl.ds(src_row, groups), :],
                                     x_scr.at[s, pl.ds(dst_row, groups), :], gsem.at[s])

    def wait_tile(s):
        def wait(r, c):
            copy(s, 0, r).wait()
            return c
        lax.fori_loop(0, rows, wait, 0, unroll=8)

    def prefetch_next():
        for k in range(GATHER_ROWS_PER_STEP):
            r = j * GATHER_ROWS_PER_STEP + k
            copy(1 - slot, srcn_ref[0, r], r).start()

    @pl.when((i == 0) & (j == 0))
    def _():
        def start(r, c):
            copy(0, src0_ref[0, r], r).start()
            return c
        lax.fori_loop(0, rows, start, 0, unroll=8)

    @pl.when(j == 0)
    def _():
        wait_tile(slot)
        for s in range(groups):
            xb_scr[:, s * LANES:(s + 1) * LANES] = x_scr[slot, pl.ds(s, tm, stride=groups), :].astype(BF16)
        o_ref[...] = jnp.zeros_like(o_ref)

    @pl.when(i < nused_ref[0])
    def _():
        xb = xb_scr[...]
        gate = _dot(xb, wg_ref[...])
        up = _dot(xb, wu_ref[...])
        prefetch_next()
        act = (gate * _sigmoid(gate) * up).astype(BF16)
        o_ref[...] += _dot(act, wd_ref[...])

    @pl.when(i >= nused_ref[0])
    def _():
        prefetch_next()

    @pl.when((i == nt - 1) & (j == nf - 1))
    def _():
        wait_tile(1 - slot)


def _experts_gather(tile_expert, nused, src, h3, wg, wu, wd):
    nt = src.shape[0] - 1
    nf = D_FF_EXPERT // TF_E
    groups = D_MODEL // LANES
    rows = nf * GATHER_ROWS_PER_STEP

    def wcol(i, j, te, nu):
        return (te[i], 0, jnp.where(i < nu[0], j, nf - 1))

    def wrow(i, j, te, nu):
        return (te[i], jnp.where(i < nu[0], j, nf - 1), 0)

    def smem_tile(index_map):
        return pl.BlockSpec((None, 1, src.shape[2]), index_map, memory_space=pltpu.SMEM)

    grid_spec = pltpu.PrefetchScalarGridSpec(
        num_scalar_prefetch=2,
        grid=(nt, nf),
        in_specs=[smem_tile(lambda i, j, te, nu: (0, 0, 0)), smem_tile(lambda i, j, te, nu: (i + 1, 0, 0)),
                  pl.BlockSpec(memory_space=pl.ANY),
                  pl.BlockSpec((None, D_MODEL, TF_E), wcol), pl.BlockSpec((None, D_MODEL, TF_E), wcol),
                  pl.BlockSpec((None, TF_E, D_MODEL), wrow)],
        out_specs=pl.BlockSpec((TM_E, D_MODEL), lambda i, j, te, nu: (i, 0)),
        scratch_shapes=[pltpu.VMEM((2, rows * groups, LANES), F32), pltpu.VMEM((TM_E, D_MODEL), BF16),
                        pltpu.SemaphoreType.DMA((2,))],
    )
    return pl.pallas_call(
        _experts_gather_kernel,
        grid_spec=grid_spec,
        out_shape=jax.ShapeDtypeStruct((nt * TM_E, D_MODEL), F32),
        compiler_params=_cparams(("arbitrary", "arbitrary"), 48),
        name="experts",
    )(tile_expert, nused, src, src, h3, wg, wu, wd)


def _combine_kernel(pos0_ref, posn_ref, x_ref, meta_ref, g_ref, y_hbm, o_ref, y_scr, sem):
    tm = x_ref.shape[0]
    slot = _gather_rows(pos0_ref, posn_ref, y_hbm, y_scr, sem)
    meta = meta_ref[...]
    y = x_ref[...] + meta[:, 2:3] * y_scr[slot, 0:tm, :] + meta[:, 3:4] * y_scr[slot, tm:2 * tm, :]
    o_ref[...] = _rms(y, g_ref[...])


def _combine(pos, x3, y, meta, g):
    t = x3.shape[0]
    tm = TM_FIN
    nt = t // tm
    row = lambda n: pl.BlockSpec((tm, n), lambda i: (i, 0))
    return pl.pallas_call(
        _combine_kernel,
        grid=(nt,),
        in_specs=_gather_specs(pos, nt) + [row(D_MODEL), row(LANES), _const_spec(g.shape),
                                           pl.BlockSpec(memory_space=pl.ANY)],
        out_specs=row(D_MODEL),
        out_shape=jax.ShapeDtypeStruct((t, D_MODEL), F32),
        scratch_shapes=[pltpu.VMEM((2, 2 * tm, D_MODEL), F32), pltpu.SemaphoreType.DMA((2,))],
        compiler_params=_cparams(("arbitrary",), 56),
        name="combine_final_norm",
    )(pos, pos, x3, meta, g, y)


def _route(meta, cnt, t):
    idx = meta[:, 0:2].astype(I32)
    rank = meta[:, 4:6].astype(I32)
    counts = cnt[0, :N_EXPERTS].astype(I32)
    tiles_e = (counts + TM_E - 1) // TM_E
    tile_end = jnp.cumsum(tiles_e)
    row_off = (tile_end - tiles_e) * TM_E
    nt = 2 * t // TM_E + N_EXPERTS
    p = nt * TM_E
    pos = row_off[idx] + rank
    src = (jnp.arange(p, dtype=I32) % t).at[pos.reshape(-1)].set(jnp.repeat(jnp.arange(t, dtype=I32), 2),
                                                                 unique_indices=True)
    src = src.reshape(nt, TM_E)
    extra_cols = (src[:, :EXPERT_SRC_WIDTH - TM_E] + t // 2) % t
    src = jnp.concatenate([src, extra_cols], axis=1)
    src = jnp.concatenate([src, (jnp.arange(EXPERT_SRC_WIDTH, dtype=I32) % t)[None, :]], axis=0)
    src = src.reshape(nt + 1, 1, EXPERT_SRC_WIDTH)
    tiles = jnp.arange(nt, dtype=I32)
    tile_expert = jnp.minimum(jnp.sum((tiles[:, None] >= tile_end[None, :]).astype(I32), axis=1),
                              N_EXPERTS - 1)
    nused = tile_end[-1:].astype(I32)
    pos_tiles = pos.reshape(t // TM_FIN, TM_FIN, 2).transpose(0, 2, 1).reshape(t // TM_FIN, 1, 2 * TM_FIN)
    return tile_expert, nused, src, pos_tiles


def kernel(x, l0_norm1_g, l0_w_in, l0_b_f, l0_dw_w, l0_dw_b, l0_cln_g, l0_cln_b, l0_w_out, l0_norm2_g, l0_w_gate, l0_w_up, l0_w_down, l1_norm1_g, l1_w_in, l1_b_i, l1_b_f, l1_conv_w, l1_conv_b, l1_wq_head, l1_wk_head, l1_hnorm_g, l1_w_out, l1_norm2_g, l1_router, l1_e_gate, l1_e_up, l1_e_down, final_norm_g):
    batch, seq, d = x.shape
    t = batch * seq
    xf = x.reshape(t, d)
    vec = lambda a: a.reshape(1, -1).astype(F32)
    bf = lambda a: a.astype(BF16)

    fw = FOX_WIDTH
    o_f = 3 * fw
    o_a = o_f + FOX_HEADS
    o_g = o_a + CONV_CHANNELS
    w_qv = jnp.concatenate([l0_w_in[:, :fw], l0_w_in[:, 2 * fw:o_f]], axis=1)
    dw_w = jnp.concatenate([l0_dw_w, jnp.zeros((1, CONV_CHANNELS), F32)], axis=0)
    q, kta, v, uc = _l0_in(
        xf, vec(l0_norm1_g), bf(w_qv), bf(l0_w_in[:, fw:2 * fw].T), bf(l0_w_in[:, o_f:o_a].T),
        l0_b_f.reshape(-1, 1).astype(F32), bf(l0_w_in[:, o_a:o_g]), bf(l0_w_in[:, o_g:]),
        dw_w, vec(l0_dw_b), vec(l0_cln_g), vec(l0_cln_b), batch=batch)
    att = _fox_attention(q, kta, v, batch=batch)
    x2 = _l0_out_ffn(xf, att, uc, bf(l0_w_out[:fw]), bf(l0_w_out[fw:]), vec(l0_norm2_g),
                     bf(l0_w_gate), bf(l0_w_up), bf(l0_w_down))

    w = MLSTM_WIDTH
    nh = MLSTM_HEADS
    w_i = l1_w_in[:, 3 * w:3 * w + nh]
    w_f = l1_w_in[:, 3 * w + nh:]
    wfc = bf(jnp.pad(w_f, ((0, 0), (0, LANES - nh))))
    bfc = jnp.pad(l1_b_f, (0, LANES - nh)).reshape(1, LANES).astype(F32)
    wgr = bf(jnp.concatenate([w_i.T, w_f.T], axis=0))
    bgr = jnp.concatenate([l1_b_i, l1_b_f]).reshape(-1, 1).astype(F32)
    ql, kt, vl, og, lfc, gr = _l1_in(
        x2, vec(l1_norm1_g), bf(l1_w_in[:, :3 * w]), wfc, bfc, wgr, bgr, l1_conv_w.astype(F32),
        vec(l1_conv_b), bf(l1_wq_head), bf(jnp.swapaxes(l1_wk_head, 1, 2)), batch=batch)
    hc = _mlstm(ql, kt, vl, og, lfc, gr, vec(l1_hnorm_g), batch=batch)
    wr = jnp.pad(l1_router.astype(F32), ((0, 0), (0, LANES - N_EXPERTS)))
    x3, h3, meta, cnt = _l1_out_router(x2, hc, bf(l1_w_out), vec(l1_norm2_g), wr)

    tile_expert, nused, src, pos_tiles = _route(meta, cnt, t)
    y = _experts_gather(tile_expert, nused, src, h3, bf(l1_e_gate), bf(l1_e_up), bf(l1_e_down))
    out = _combine(pos_tiles, x3, y, meta, vec(final_norm_g))
    return out.reshape(batch, seq, d)
```

```python
import functools

import jax
import jax.numpy as jnp
from jax import lax
from jax.experimental import pallas as pl
from jax.experimental.pallas import tpu as pltpu

F32 = jnp.float32
BF16 = jnp.bfloat16
I32 = jnp.int32

EPS = 1e-6
NEG = -1e30
LOG2E = 1.4426950408889634

D_MODEL = 1024
FOX_HEADS = 8
FOX_HEAD_DIM = 64
FOX_WIDTH = FOX_HEADS * FOX_HEAD_DIM
CONV_CHANNELS = 512
CONV_WIDTH = 31
MLSTM_HEADS = 8
MLSTM_HEAD_DIM = 128
MLSTM_WIDTH = MLSTM_HEADS * MLSTM_HEAD_DIM
MLSTM_CONV_WIDTH = 4
D_FF = 2816
N_EXPERTS = 8
D_FF_EXPERT = 3584

LANES = 128
SUBLANES = 8
MIB = 1024 * 1024

TM_IN0 = 512
TQ = 512
FOX_Q_HALVES = 4
FOX_KAUG = 256
CONV_ROWS = 64
CONV_HALO = 32
TM_FFN0 = 512
TM_IN1 = 512
L1_HALO = 8
CHUNK = 512
TM_OUT1 = 512
TM_E = 512
TF_E = 512
GATHER_ROWS_PER_STEP = 80
EXPERT_SRC_WIDTH = 640
TM_FIN = 512


def _cparams(sem, vmem_mib):
    return pltpu.CompilerParams(dimension_semantics=sem, vmem_limit_bytes=vmem_mib * MIB)


def _rms(x, g):
    return x * lax.rsqrt(jnp.mean(x * x, axis=-1, keepdims=True) + EPS) * g


def _sigmoid(x):
    return 1.0 / (1.0 + jnp.exp(-x))


def _log_sigmoid(x):
    return jnp.minimum(x, 0.0) - jnp.log(1.0 + jnp.exp(-jnp.abs(x)))


def _dot(a, b):
    return jnp.dot(a, b, preferred_element_type=F32)


def _dot_nt(a, b):
    return lax.dot_general(a, b, (((1,), (1,)), ((), ())), preferred_element_type=F32)


def _split3(x):
    hi = x.astype(BF16)
    r1 = x - hi.astype(F32)
    mid = r1.astype(BF16)
    lo = (r1 - mid.astype(F32)).astype(BF16)
    return hi, mid, lo


def _const_spec(shape):
    nd = len(shape)
    return pl.BlockSpec(shape, lambda *_: (0,) * nd, pipeline_mode=pl.Buffered(1))


def _l0_in_kernel(x_ref, g_ref, wqv_ref, wkt_ref, wft_ref, bf_ref, wa_ref, wg_ref,
                  cw_ref, cb_ref, cg_ref, cbeta_ref,
                  q_ref, kta_ref, v_ref, uc_ref, carry_ref, win_ref, sh_ref, *, tiles_per_batch):
    i = pl.program_id(0)
    tm = x_ref.shape[0]

    @pl.when(i % tiles_per_batch == 0)
    def _():
        carry_ref[...] = jnp.zeros_like(carry_ref)
        win_ref[0:CONV_HALO, :] = jnp.zeros((CONV_HALO, CONV_CHANNELS), F32)

    hb = _rms(x_ref[...], g_ref[...]).astype(BF16)
    win_ref[CONV_HALO:CONV_HALO + tm, :] = _dot(hb, wa_ref[...]) * _sigmoid(_dot(hb, wg_ref[...]))
    _conv_ln_swish(win_ref, sh_ref, cw_ref, cb_ref, cg_ref, cbeta_ref, uc_ref)

    qv = _dot(hb, wqv_ref[...])
    q_ref[...] = (qv[:, :FOX_WIDTH] * (FOX_HEAD_DIM ** -0.5 * LOG2E)).astype(BF16)
    v_ref[...] = qv[:, FOX_WIDTH:].astype(BF16)
    kt = _dot_nt(wkt_ref[...], hb).astype(BF16)

    logf = _log_sigmoid(_dot_nt(wft_ref[...], hb) + bf_ref[...])
    lane = lax.broadcasted_iota(I32, logf.shape, 1)
    cum = logf
    shift = 1
    while shift < tm:
        cum = cum + jnp.where(lane >= shift, pltpu.roll(cum, shift, 1), 0.0)
        shift *= 2

    c = cum + carry_ref[:, 0:1]
    carry_ref[...] = jnp.broadcast_to(c[:, tm - 1:tm], carry_ref.shape)

    pieces = [p.astype(F32) for p in _split3(c * LOG2E)]
    sub = lax.broadcasted_iota(I32, (2 * SUBLANES, tm), 0)
    zeros_tail = jnp.zeros((FOX_KAUG - LANES - 2 * SUBLANES, tm), BF16)
    for p in range(FOX_HEADS // 2):
        ext = jnp.zeros((2 * SUBLANES, tm), F32)
        for hh in range(2):
            for n, piece in enumerate(pieces):
                ext = jnp.where(sub == 3 * hh + n, piece[2 * p + hh:2 * p + hh + 1, :], ext)
        kta_ref[p, 0:LANES, :] = kt[p * LANES:(p + 1) * LANES, :]
        kta_ref[p, LANES:LANES + 2 * SUBLANES, :] = ext.astype(BF16)
        kta_ref[p, LANES + 2 * SUBLANES:, :] = zeros_tail


def _l0_in(x, g, wqv, wkt, wft, bf, wa, wg, cw, cb, cg, cbeta, *, batch):
    t = x.shape[0]
    tm = TM_IN0
    tpb = (t // batch) // tm
    pairs = FOX_HEADS // 2
    row = lambda w: pl.BlockSpec((tm, w), lambda i: (i, 0))
    consts = [g, wqv, wkt, wft, bf, wa, wg, cw, cb, cg, cbeta]
    return pl.pallas_call(
        functools.partial(_l0_in_kernel, tiles_per_batch=tpb),
        grid=(t // tm,),
        in_specs=[row(D_MODEL)] + [_const_spec(a.shape) for a in consts],
        out_specs=[row(FOX_WIDTH), pl.BlockSpec((pairs, FOX_KAUG, tm), lambda i: (0, 0, i)),
                   row(FOX_WIDTH), row(CONV_CHANNELS)],
        out_shape=[jax.ShapeDtypeStruct((t, FOX_WIDTH), BF16),
                   jax.ShapeDtypeStruct((pairs, FOX_KAUG, t), BF16),
                   jax.ShapeDtypeStruct((t, FOX_WIDTH), BF16),
                   jax.ShapeDtypeStruct((t, CONV_CHANNELS), BF16)],
        scratch_shapes=[pltpu.VMEM((FOX_HEADS, LANES), F32),
                        pltpu.VMEM((CONV_HALO + tm, CONV_CHANNELS), F32),
                        pltpu.VMEM((SUBLANES - 1, CONV_HALO + tm - SUBLANES, CONV_CHANNELS), F32)],
        compiler_params=_cparams(("arbitrary",), 48),
        name="l0_in",
    )(x, *consts)


def _fox_kernel(q_ref, kta_ref, v_ref, o_ref, q_scr, m_scr, acc_scr):
    qi = pl.program_id(2)
    tq = TQ
    tk = TQ
    lane = lax.broadcasted_iota(I32, (tq, LANES), 1)
    is_a = lane < FOX_HEAD_DIM
    lane_k = lax.broadcasted_iota(I32, (tk, LANES), 1)

    ext_a = jnp.where(lane < 3, -1.0, 0.0)
    ext_b = jnp.where((lane >= 3) & (lane < 6), -1.0, 0.0)
    for half in range(FOX_Q_HALVES):
        q2 = q_ref[half * tq:(half + 1) * tq, :].astype(F32)
        q_scr[2 * half] = jnp.concatenate([jnp.where(is_a, q2, 0.0), ext_a], axis=1).astype(BF16)
        q_scr[2 * half + 1] = jnp.concatenate([jnp.where(is_a, 0.0, q2), ext_b], axis=1).astype(BF16)
    m_scr[...] = jnp.full(m_scr.shape, NEG, F32)
    acc_scr[...] = jnp.zeros_like(acc_scr)

    def tile(ki, plan):
        start = pl.multiple_of(ki * tk, tk)
        kt = kta_ref[:, pl.ds(start, tk)]
        vf = v_ref[pl.ds(start, tk), :].astype(F32)
        v_augs = (jnp.where(lane_k < FOX_HEAD_DIM, vf, 1.0).astype(BF16),
                  jnp.where(lane_k < FOX_HEAD_DIM, 1.0, vf).astype(BF16))
        causal = (lax.broadcasted_iota(I32, (tq, tk), 1) <= lax.broadcasted_iota(I32, (tq, tk), 0))
        chains = [(2 * half + h, h, masked) for half, masked in plan for h in range(2)]
        logits = [_dot(q_scr[n], kt) for n, _, _ in chains]
        probs, alphas = [], []
        for (n, _, masked), s in zip(chains, logits):
            if masked:
                s = jnp.where(causal, s, NEG)
            m_old = m_scr[n]
            m_new = jnp.maximum(m_old, jnp.max(s, axis=1, keepdims=True))
            probs.append(jnp.concatenate(
                [jnp.exp2(s[:, j * LANES:(j + 1) * LANES] - m_new).astype(BF16) for j in range(tk // LANES)],
                axis=1))
            alphas.append(jnp.exp2(m_old - m_new))
            m_scr[n] = m_new
        for (n, h, _), p, alpha in zip(chains, probs, alphas):
            acc_scr[n] = acc_scr[n] * alpha + _dot(p, v_augs[h])

    def body(ki, carry):
        tile(ki, tuple((half, False) for half in range(FOX_Q_HALVES)))
        return carry

    first_diag = FOX_Q_HALVES * qi
    lax.fori_loop(0, first_diag, body, 0)
    for d in range(FOX_Q_HALVES):
        tile(first_diag + d, ((d, True),) + tuple((half, False) for half in range(d + 1, FOX_Q_HALVES)))

    for half in range(FOX_Q_HALVES):
        acc_a = acc_scr[2 * half]
        acc_b = acc_scr[2 * half + 1]
        o = jnp.where(is_a, acc_a / pltpu.roll(acc_a, FOX_HEAD_DIM, 1),
                      acc_b / pltpu.roll(acc_b, FOX_HEAD_DIM, 1))
        o_ref[half * tq:(half + 1) * tq, :] = o.astype(o_ref.dtype)


def _fox_attention(q, kta, v, *, batch):
    t = q.shape[0]
    seq = t // batch
    rows = FOX_Q_HALVES * TQ
    nq = seq // rows
    pairs = FOX_HEADS // 2
    nstate = 2 * FOX_Q_HALVES
    return pl.pallas_call(
        _fox_kernel,
        grid=(batch, pairs, nq),
        in_specs=[pl.BlockSpec((rows, LANES), lambda b, p, i: (b * nq + i, p)),
                  pl.BlockSpec((None, FOX_KAUG, seq), lambda b, p, i: (p, 0, b)),
                  pl.BlockSpec((seq, LANES), lambda b, p, i: (b, p))],
        out_specs=pl.BlockSpec((rows, LANES), lambda b, p, i: (b * nq + i, p)),
        out_shape=jax.ShapeDtypeStruct((t, FOX_WIDTH), BF16),
        scratch_shapes=[pltpu.VMEM((nstate, TQ, FOX_KAUG), BF16),
                        pltpu.VMEM((nstate, TQ, LANES), F32), pltpu.VMEM((nstate, TQ, LANES), F32)],
        compiler_params=_cparams(("arbitrary", "arbitrary", "arbitrary"), 40),
        name="fox_attention",
    )(q, kta, v)


def _conv_ln_swish(win_ref, sh_ref, w_ref, b_ref, g_ref, beta_ref, o_ref):
    tm = o_ref.shape[0]
    n_sh = sh_ref.shape[1]
    for b in range(1, SUBLANES):
        sh_ref[b - 1] = win_ref[b:b + n_sh, :]
    first = CONV_HALO - (CONV_WIDTH - 1)
    for c in range(tm // CONV_ROWS):
        r0 = c * CONV_ROWS
        acc = jnp.broadcast_to(b_ref[...], (CONV_ROWS, CONV_CHANNELS))
        for j in range(CONV_WIDTH):
            b = (first + j) % SUBLANES
            a = r0 + first + j - b
            tap = win_ref[a:a + CONV_ROWS, :] if b == 0 else sh_ref[b - 1, a:a + CONV_ROWS, :]
            acc = acc + tap * w_ref[j:j + 1, :]
        mu = jnp.mean(acc, axis=-1, keepdims=True)
        d = acc - mu
        var = jnp.mean(d * d, axis=-1, keepdims=True)
        y = d * lax.rsqrt(var + EPS) * g_ref[...] + beta_ref[...]
        o_ref[r0:r0 + CONV_ROWS, :] = (y * _sigmoid(y)).astype(o_ref.dtype)
    win_ref[0:CONV_HALO, :] = win_ref[tm:tm + CONV_HALO, :]


def _l0_out_ffn_kernel(x_ref, att_ref, u_ref, woa_ref, wou_ref, g_ref, wg_ref, wu_ref, wd_ref, o_ref):
    x1 = x_ref[...] + _dot(att_ref[...], woa_ref[...]) + _dot(u_ref[...], wou_ref[...])
    hb = _rms(x1, g_ref[...]).astype(BF16)
    gate = _dot(hb, wg_ref[...])
    act = (gate * _sigmoid(gate) * _dot(hb, wu_ref[...])).astype(BF16)
    o_ref[...] = x1 + _dot(act, wd_ref[...])


def _l0_out_ffn(x, att, u, woa, wou, g, wg, wu, wd):
    t = x.shape[0]
    tm = TM_FFN0
    row = lambda w: pl.BlockSpec((tm, w), lambda i: (i, 0))
    return pl.pallas_call(
        _l0_out_ffn_kernel,
        grid=(t // tm,),
        in_specs=[row(D_MODEL), row(FOX_WIDTH), row(CONV_CHANNELS), _const_spec(woa.shape),
                  _const_spec(wou.shape), _const_spec(g.shape), _const_spec(wg.shape),
                  _const_spec(wu.shape), _const_spec(wd.shape)],
        out_specs=row(D_MODEL),
        out_shape=jax.ShapeDtypeStruct((t, D_MODEL), F32),
        compiler_params=_cparams(("arbitrary",), 56),
        name="l0_out_ffn",
    )(x, att, u, woa, wou, g, wg, wu, wd)


def _l1_in_kernel(x_ref, g_ref, wuvo_ref, wfc_ref, bfc_ref, wgr_ref, bgr_ref, cw_ref, cb_ref,
                  wq_ref, wkt_ref, q_ref, kt_ref, v_ref, og_ref, lfc_ref, gr_ref, win_ref,
                  *, tiles_per_batch):
    i = pl.program_id(0)
    tm = x_ref.shape[0]
    w = MLSTM_WIDTH
    hb = _rms(x_ref[...], g_ref[...]).astype(BF16)
    uvo = _dot(hb, wuvo_ref[...])
    v_ref[...] = uvo[:, w:2 * w].astype(BF16)
    og_ref[...] = _sigmoid(uvo[:, 2 * w:]).astype(BF16)

    lfc_ref[...] = _log_sigmoid(_dot(hb, wfc_ref[...]) + bfc_ref[...])
    grow = _dot_nt(wgr_ref[...], hb) + bgr_ref[...]
    is_i = lax.broadcasted_iota(I32, grow.shape, 0) < MLSTM_HEADS
    gr_ref[...] = jnp.where(is_i, grow, _log_sigmoid(grow))

    @pl.when(i % tiles_per_batch == 0)
    def _():
        win_ref[0:L1_HALO, :] = jnp.zeros((L1_HALO, w), F32)

    win_ref[L1_HALO:L1_HALO + tm, :] = uvo[:, :w]
    first = L1_HALO - (MLSTM_CONV_WIDTH - 1)
    acc = jnp.broadcast_to(cb_ref[...], (tm, w))
    for j in range(MLSTM_CONV_WIDTH):
        acc = acc + win_ref[first + j:first + j + tm, :] * cw_ref[j:j + 1, :]
    win_ref[0:L1_HALO, :] = win_ref[tm:tm + L1_HALO, :]
    uc = (acc * _sigmoid(acc)).astype(BF16)
    d = MLSTM_HEAD_DIM
    for h in range(MLSTM_HEADS):
        uh = uc[:, h * d:(h + 1) * d]
        q_ref[:, h * d:(h + 1) * d] = _dot(uh, wq_ref[h]).astype(BF16)
        kt_ref[h * d:(h + 1) * d, :] = (_dot_nt(wkt_ref[h], uh) * (d ** -0.5)).astype(BF16)


def _l1_in(x, g, wuvo, wfc, bfc, wgr, bgr, cw, cb, wq, wkt, *, batch):
    t = x.shape[0]
    tm = TM_IN1
    tpb = (t // batch) // tm
    w = MLSTM_WIDTH
    row = lambda n: pl.BlockSpec((tm, n), lambda i: (i, 0))
    col = lambda n: pl.BlockSpec((n, tm), lambda i: (0, i))
    consts = [g, wuvo, wfc, bfc, wgr, bgr, cw, cb, wq, wkt]
    return pl.pallas_call(
        functools.partial(_l1_in_kernel, tiles_per_batch=tpb),
        grid=(t // tm,),
        in_specs=[row(D_MODEL)] + [_const_spec(a.shape) for a in consts],
        out_specs=[row(w), col(w), row(w), row(w), row(LANES), col(2 * MLSTM_HEADS)],
        out_shape=[jax.ShapeDtypeStruct((t, w), BF16), jax.ShapeDtypeStruct((w, t), BF16),
                   jax.ShapeDtypeStruct((t, w), BF16), jax.ShapeDtypeStruct((t, w), BF16),
                   jax.ShapeDtypeStruct((t, LANES), F32),
                   jax.ShapeDtypeStruct((2 * MLSTM_HEADS, t), F32)],
        scratch_shapes=[pltpu.VMEM((L1_HALO + tm, w), F32)],
        compiler_params=_cparams(("arbitrary",), 48),
        name="l1_in",
    )(x, *consts)


def _mlstm_kernel(q_ref, kt_ref, v_ref, og_ref, lfc_ref, gr_ref, hg_ref, o_ref, c_scr, m_scr):
    ci = pl.program_id(1)
    L = q_ref.shape[0]
    d = MLSTM_HEAD_DIM
    nh = MLSTM_HEADS

    @pl.when(ci == 0)
    def _():
        c_scr[...] = jnp.zeros_like(c_scr)
        m_scr[...] = jnp.zeros_like(m_scr)

    r = lax.broadcasted_iota(I32, (L, L), 0)
    c = lax.broadcasted_iota(I32, (L, L), 1)
    tril = r >= c
    tri_lo = jnp.where(tril, 1.0, 0.0).astype(BF16)
    tri_up = jnp.where(r <= c, 1.0, 0.0).astype(BF16)

    bc_col = sum(_dot(tri_lo, p) for p in _split3(lfc_ref[...]))
    bc_row = sum(_dot(p, tri_up) for p in _split3(gr_ref[nh:2 * nh, :]))
    ones_blk = jnp.ones((L, d), BF16)
    heads = range(nh)
    sls = [slice(h * d, (h + 1) * d) for h in heads]

    cts = [c_scr[h] for h in heads]
    v_augs = [jnp.concatenate([v_ref[:, sls[h]], ones_blk], axis=1) for h in heads]
    qk = [_dot(q_ref[:, sls[h]], kt_ref[sls[h], :]) for h in heads]
    qc = [_dot(q_ref[:, sls[h]], cts[h].astype(BF16)) for h in heads]

    w_intra, w_inter, m_ts, new_state = [], [], [], []
    for h in heads:
        bcol = bc_col[:, h:h + 1]
        brow = bc_row[h:h + 1, :]
        srow = gr_ref[h:h + 1, :] - brow
        g = brow[:, L - 1:L]
        m_old = m_scr[h][:, 0:1]
        dmat = jnp.where(tril, bcol + srow, NEG)
        inter = bcol + m_old
        m_t = jnp.maximum(inter, jnp.max(dmat, axis=1, keepdims=True))
        w_inter.append(jnp.exp(inter - m_t))
        w_intra.append((jnp.exp(dmat - m_t) * qk[h]).astype(BF16))
        m_ts.append(m_t)
        a_row = g + srow
        m_new = jnp.maximum(g + m_old, jnp.max(a_row, axis=1, keepdims=True))
        kw = (kt_ref[sls[h], :].astype(F32) * jnp.exp(a_row - m_new)).astype(BF16)
        new_state.append((jnp.exp(g + m_old - m_new), kw, m_new))

    for h in heads:
        res = w_inter[h] * qc[h] + _dot(w_intra[h], v_augs[h])
        den = jnp.maximum(jnp.abs(res[:, d:]), jnp.exp(-m_ts[h]))
        hc = og_ref[:, sls[h]].astype(F32) * (res[:, :d] / den)
        o_ref[:, sls[h]] = _rms(hc, hg_ref[:, sls[h]]).astype(o_ref.dtype)

    for h in heads:
        decay, kw, m_new = new_state[h]
        c_scr[h] = decay * cts[h] + _dot(kw, v_augs[h])
        m_scr[h] = jnp.broadcast_to(m_new, (1, LANES))


def _mlstm(q, kt, v, og, lfc, gr, hg, *, batch):
    t = q.shape[0]
    w = MLSTM_WIDTH
    nc = (t // batch) // CHUNK
    row = lambda n: pl.BlockSpec((CHUNK, n), lambda b, c: (b * nc + c, 0))
    col = lambda n: pl.BlockSpec((n, CHUNK), lambda b, c: (0, b * nc + c))
    return pl.pallas_call(
        _mlstm_kernel,
        grid=(batch, nc),
        in_specs=[row(w), col(w), row(w), row(w), row(LANES), col(2 * MLSTM_HEADS), _const_spec(hg.shape)],
        out_specs=row(w),
        out_shape=jax.ShapeDtypeStruct((t, w), BF16),
        scratch_shapes=[pltpu.VMEM((MLSTM_HEADS, MLSTM_HEAD_DIM, 2 * MLSTM_HEAD_DIM), F32),
                        pltpu.VMEM((MLSTM_HEADS, 1, LANES), F32)],
        compiler_params=_cparams(("arbitrary", "arbitrary"), 32),
        name="mlstm",
    )(q, kt, v, og, lfc, gr, hg)


def _l1_out_router_kernel(x_ref, hc_ref, wo_ref, g_ref, wr_ref, x3_ref, h3_ref, meta_ref, cnt_ref, carry_ref):
    i = pl.program_id(0)
    tm = x_ref.shape[0]

    @pl.when(i == 0)
    def _():
        carry_ref[...] = jnp.zeros_like(carry_ref)

    x3 = x_ref[...] + _dot(hc_ref[...], wo_ref[...])
    x3_ref[...] = x3
    h3 = _rms(x3, g_ref[...])
    for s in range(D_MODEL // LANES):
        h3_ref[pl.ds(s, tm, stride=D_MODEL // LANES), :] = h3[:, s * LANES:(s + 1) * LANES]

    h_hi, h_mid, _ = _split3(h3)
    w_hi, w_mid, _ = _split3(wr_ref[...])
    logits = _dot(h_hi, w_hi) + (_dot(h_hi, w_mid) + _dot(h_mid, w_hi))

    lane = lax.broadcasted_iota(I32, (tm, LANES), 1)
    lane_f = lane.astype(F32)
    lg = jnp.where(lane < N_EXPERTS, logits, NEG)
    v1 = jnp.max(lg, axis=1, keepdims=True)
    i1 = jnp.min(jnp.where(lg == v1, lane_f, float(LANES)), axis=1, keepdims=True)
    lg2 = jnp.where(lane_f == i1, NEG, lg)
    v2 = jnp.max(lg2, axis=1, keepdims=True)
    i2 = jnp.min(jnp.where(lg2 == v2, lane_f, float(LANES)), axis=1, keepdims=True)
    e = jnp.exp(v2 - v1)
    g1 = 1.0 / (1.0 + e)
    g2 = e / (1.0 + e)

    oh1 = lane_f == i1
    oh2 = lane_f == i2
    oh = jnp.where(oh1 | oh2, 1.0, 0.0)
    r = lax.broadcasted_iota(I32, (tm, tm), 0)
    c = lax.broadcasted_iota(I32, (tm, tm), 1)
    strict = jnp.where(c < r, 1.0, 0.0).astype(BF16)
    pos = _dot(strict, oh.astype(BF16)) + carry_ref[0:1, :]
    rank1 = jnp.sum(jnp.where(oh1, pos, 0.0), axis=1, keepdims=True)
    rank2 = jnp.sum(jnp.where(oh2, pos, 0.0), axis=1, keepdims=True)
    total = carry_ref[0:1, :] + jnp.sum(oh, axis=0, keepdims=True)
    carry_ref[...] = jnp.broadcast_to(total, carry_ref.shape)
    cnt_ref[...] = jnp.broadcast_to(total, cnt_ref.shape)

    meta = jnp.zeros((tm, LANES), F32)
    for k, val in enumerate((i1, i2, g1, g2, rank1, rank2)):
        meta = jnp.where(lane == k, val, meta)
    meta_ref[...] = meta


def _l1_out_router(x, hc, wo, g, wr):
    t = x.shape[0]
    tm = TM_OUT1
    groups = D_MODEL // LANES
    row = lambda n: pl.BlockSpec((tm, n), lambda i: (i, 0))
    return pl.pallas_call(
        _l1_out_router_kernel,
        grid=(t // tm,),
        in_specs=[row(D_MODEL), row(MLSTM_WIDTH), _const_spec(wo.shape), _const_spec(g.shape),
                  _const_spec(wr.shape)],
        out_specs=[row(D_MODEL), pl.BlockSpec((tm * groups, LANES), lambda i: (i, 0)), row(LANES),
                   pl.BlockSpec((SUBLANES, LANES), lambda i: (0, 0))],
        out_shape=[jax.ShapeDtypeStruct((t, D_MODEL), F32), jax.ShapeDtypeStruct((t * groups, LANES), F32),
                   jax.ShapeDtypeStruct((t, LANES), F32), jax.ShapeDtypeStruct((SUBLANES, LANES), F32)],
        scratch_shapes=[pltpu.VMEM((SUBLANES, LANES), F32)],
        compiler_params=_cparams(("arbitrary",), 32),
        name="l1_out_router",
    )(x, hc, wo, g, wr)


def _gather_rows(tab0_ref, tabn_ref, src_hbm, buf, sem):
    i = pl.program_id(0)
    n = pl.num_programs(0)
    rows = buf.shape[1]
    slot = i % 2

    def row_copy(s, src_row, r):
        return pltpu.make_async_copy(src_hbm.at[pl.ds(src_row, 1), :], buf.at[s, pl.ds(r, 1), :], sem.at[s])

    @pl.when(i == 0)
    def _():
        def start(r, c):
            row_copy(0, tab0_ref[0, r], r).start()
            return c
        lax.fori_loop(0, rows, start, 0, unroll=8)

    @pl.when(i + 1 < n)
    def _():
        for r in range(rows):
            row_copy(1 - slot, tabn_ref[0, r], r).start()

    def wait(r, c):
        row_copy(slot, 0, r).wait()
        return c
    lax.fori_loop(0, rows, wait, 0, unroll=8)
    return slot


def _gather_specs(table, n):
    width = table.shape[2]

    def smem_tile(index_map):
        return pl.BlockSpec((None, 1, width), index_map, memory_space=pltpu.SMEM)

    return [smem_tile(lambda i: (0, 0, 0)), smem_tile(lambda i: (jnp.minimum(i + 1, n - 1), 0, 0))]


def _experts_gather_kernel(te_ref, nused_ref, src0_ref, srcn_ref, h_hbm, wg_ref, wu_ref, wd_ref, o_ref,
                           x_scr, xb_scr, gsem):
    i = pl.program_id(0)
    j = pl.program_id(1)
    nt = pl.num_programs(0)
    nf = pl.num_programs(1)
    tm = o_ref.shape[0]
    groups = D_MODEL // LANES
    rows = x_scr.shape[1] // groups
    slot = i % 2

    def copy(s, tok, r):
        src_row = tok * groups
        dst_row = r * groups
        if not isinstance(tok, int):
            src_row = pl.multiple_of(src_row, groups)
        if not isinstance(r, int):
            dst_row = pl.multiple_of(dst_row, groups)
        return pltpu.make_async_copy(h_hbm.at[pl.ds(src_row, groups), :],
                                     x_scr.at[s, pl.ds(dst_row, groups), :], gsem.at[s])

    def wait_tile(s):
        def wait(r, c):
            copy(s, 0, r).wait()
            return c
        lax.fori_loop(0, rows, wait, 0, unroll=8)

    def prefetch_next():
        for k in range(GATHER_ROWS_PER_STEP):
            r = j * GATHER_ROWS_PER_STEP + k
            copy(1 - slot, srcn_ref[0, r], r).start()

    @pl.when((i == 0) & (j == 0))
    def _():
        def start(r, c):
            copy(0, src0_ref[0, r], r).start()
            return c
        lax.fori_loop(0, rows, start, 0, unroll=8)

    @pl.when(j == 0)
    def _():
        wait_tile(slot)
        for s in range(groups):
            xb_scr[:, s * LANES:(s + 1) * LANES] = x_scr[slot, pl.ds(s, tm, stride=groups), :].astype(BF16)
        o_ref[...] = jnp.zeros_like(o_ref)

    @pl.when(i < nused_ref[0])
    def _():
        xb = xb_scr[...]
        gate = _dot(xb, wg_ref[...])
        up = _dot(xb, wu_ref[...])
        prefetch_next()
        act = (gate * _sigmoid(gate) * up).astype(BF16)
        o_ref[...] += _dot(act, wd_ref[...])

    @pl.when(i >= nused_ref[0])
    def _():
        prefetch_next()

    @pl.when((i == nt - 1) & (j == nf - 1))
    def _():
        wait_tile(1 - slot)


def _experts_gather(tile_expert, nused, src, h3, wg, wu, wd):
    nt = src.shape[0] - 1
    nf = D_FF_EXPERT // TF_E
    groups = D_MODEL // LANES
    rows = nf * GATHER_ROWS_PER_STEP

    def wcol(i, j, te, nu):
        return (te[i], 0, jnp.where(i < nu[0], j, nf - 1))

    def wrow(i, j, te, nu):
        return (te[i], jnp.where(i < nu[0], j, nf - 1), 0)

    def smem_tile(index_map):
        return pl.BlockSpec((None, 1, src.shape[2]), index_map, memory_space=pltpu.SMEM)

    grid_spec = pltpu.PrefetchScalarGridSpec(
        num_scalar_prefetch=2,
        grid=(nt, nf),
        in_specs=[smem_tile(lambda i, j, te, nu: (0, 0, 0)), smem_tile(lambda i, j, te, nu: (i + 1, 0, 0)),
                  pl.BlockSpec(memory_space=pl.ANY),
                  pl.BlockSpec((None, D_MODEL, TF_E), wcol), pl.BlockSpec((None, D_MODEL, TF_E), wcol),
                  pl.BlockSpec((None, TF_E, D_MODEL), wrow)],
        out_specs=pl.BlockSpec((TM_E, D_MODEL), lambda i, j, te, nu: (i, 0)),
        scratch_shapes=[pltpu.VMEM((2, rows * groups, LANES), F32), pltpu.VMEM((TM_E, D_MODEL), BF16),
                        pltpu.SemaphoreType.DMA((2,))],
    )
    return pl.pallas_call(
        _experts_gather_kernel,
        grid_spec=grid_spec,
        out_shape=jax.ShapeDtypeStruct((nt * TM_E, D_MODEL), F32),
        compiler_params=_cparams(("arbitrary", "arbitrary"), 48),
        name="experts",
    )(tile_expert, nused, src, src, h3, wg, wu, wd)


def _combine_kernel(pos0_ref, posn_ref, x_ref, meta_ref, g_ref, y_hbm, o_ref, y_scr, sem):
    tm = x_ref.shape[0]
    slot = _gather_rows(pos0_ref, posn_ref, y_hbm, y_scr, sem)
    meta = meta_ref[...]
    y = x_ref[...] + meta[:, 2:3] * y_scr[slot, 0:tm, :] + meta[:, 3:4] * y_scr[slot, tm:2 * tm, :]
    o_ref[...] = _rms(y, g_ref[...])


def _combine(pos, x3, y, meta, g):
    t = x3.shape[0]
    tm = TM_FIN
    nt = t // tm
    row = lambda n: pl.BlockSpec((tm, n), lambda i: (i, 0))
    return pl.pallas_call(
        _combine_kernel,
        grid=(nt,),
        in_specs=_gather_specs(pos, nt) + [row(D_MODEL), row(LANES), _const_spec(g.shape),
                                           pl.BlockSpec(memory_space=pl.ANY)],
        out_specs=row(D_MODEL),
        out_shape=jax.ShapeDtypeStruct((t, D_MODEL), F32),
        scratch_shapes=[pltpu.VMEM((2, 2 * tm, D_MODEL), F32), pltpu.SemaphoreType.DMA((2,))],
        compiler_params=_cparams(("arbitrary",), 56),
        name="combine_final_norm",
    )(pos, pos, x3, meta, g, y)


def _route(meta, cnt, t):
    idx = meta[:, 0:2].astype(I32)
    rank = meta[:, 4:6].astype(I32)
    counts = cnt[0, :N_EXPERTS].astype(I32)
    tiles_e = (counts + TM_E - 1) // TM_E
    tile_end = jnp.cumsum(tiles_e)
    row_off = (tile_end - tiles_e) * TM_E
    nt = 2 * t // TM_E + N_EXPERTS
    p = nt * TM_E
    pos = row_off[idx] + rank
    src = (jnp.arange(p, dtype=I32) % t).at[pos.reshape(-1)].set(jnp.repeat(jnp.arange(t, dtype=I32), 2),
                                                                 unique_indices=True)
    src = src.reshape(nt, TM_E)
    extra_cols = (src[:, :EXPERT_SRC_WIDTH - TM_E] + t // 2) % t
    src = jnp.concatenate([src, extra_cols], axis=1)
    src = jnp.concatenate([src, (jnp.arange(EXPERT_SRC_WIDTH, dtype=I32) % t)[None, :]], axis=0)
    src = src.reshape(nt + 1, 1, EXPERT_SRC_WIDTH)
    tiles = jnp.arange(nt, dtype=I32)
    tile_expert = jnp.minimum(jnp.sum((tiles[:, None] >= tile_end[None, :]).astype(I32), axis=1),
                              N_EXPERTS - 1)
    nused = tile_end[-1:].astype(I32)
    pos_tiles = pos.reshape(t // TM_FIN, TM_FIN, 2).transpose(0, 2, 1).reshape(t // TM_FIN, 1, 2 * TM_FIN)
    return tile_expert, nused, src, pos_tiles


def kernel(x, l0_norm1_g, l0_w_in, l0_b_f, l0_dw_w, l0_dw_b, l0_cln_g, l0_cln_b, l0_w_out, l0_norm2_g, l0_w_gate, l0_w_up, l0_w_down, l1_norm1_g, l1_w_in, l1_b_i, l1_b_f, l1_conv_w, l1_conv_b, l1_wq_head, l1_wk_head, l1_hnorm_g, l1_w_out, l1_norm2_g, l1_router, l1_e_gate, l1_e_up, l1_e_down, final_norm_g):
    batch, seq, d = x.shape
    t = batch * seq
    xf = x.reshape(t, d)
    vec = lambda a: a.reshape(1, -1).astype(F32)
    bf = lambda a: a.astype(BF16)

    fw = FOX_WIDTH
    o_f = 3 * fw
    o_a = o_f + FOX_HEADS
    o_g = o_a + CONV_CHANNELS
    w_qv = jnp.concatenate([l0_w_in[:, :fw], l0_w_in[:, 2 * fw:o_f]], axis=1)
    dw_w = jnp.concatenate([l0_dw_w, jnp.zeros((1, CONV_CHANNELS), F32)], axis=0)
    q, kta, v, uc = _l0_in(
        xf, vec(l0_norm1_g), bf(w_qv), bf(l0_w_in[:, fw:2 * fw].T), bf(l0_w_in[:, o_f:o_a].T),
        l0_b_f.reshape(-1, 1).astype(F32), bf(l0_w_in[:, o_a:o_g]), bf(l0_w_in[:, o_g:]),
        dw_w, vec(l0_dw_b), vec(l0_cln_g), vec(l0_cln_b), batch=batch)
    att = _fox_attention(q, kta, v, batch=batch)
    x2 = _l0_out_ffn(xf, att, uc, bf(l0_w_out[:fw]), bf(l0_w_out[fw:]), vec(l0_norm2_g),
                     bf(l0_w_gate), bf(l0_w_up), bf(l0_w_down))

    w = MLSTM_WIDTH
    nh = MLSTM_HEADS
    w_i = l1_w_in[:, 3 * w:3 * w + nh]
    w_f = l1_w_in[:, 3 * w + nh:]
    wfc = bf(jnp.pad(w_f, ((0, 0), (0, LANES - nh))))
    bfc = jnp.pad(l1_b_f, (0, LANES - nh)).reshape(1, LANES).astype(F32)
    wgr = bf(jnp.concatenate([w_i.T, w_f.T], axis=0))
    bgr = jnp.concatenate([l1_b_i, l1_b_f]).reshape(-1, 1).astype(F32)
    ql, kt, vl, og, lfc, gr = _l1_in(
        x2, vec(l1_norm1_g), bf(l1_w_in[:, :3 * w]), wfc, bfc, wgr, bgr, l1_conv_w.astype(F32),
        vec(l1_conv_b), bf(l1_wq_head), bf(jnp.swapaxes(l1_wk_head, 1, 2)), batch=batch)
    hc = _mlstm(ql, kt, vl, og, lfc, gr, vec(l1_hnorm_g), batch=batch)
    wr = jnp.pad(l1_router.astype(F32), ((0, 0), (0, LANES - N_EXPERTS)))
    x3, h3, meta, cnt = _l1_out_router(x2, hc, bf(l1_w_out), vec(l1_norm2_g), wr)

    tile_expert, nused, src, pos_tiles = _route(meta, cnt, t)
    y = _experts_gather(tile_expert, nused, src, h3, bf(l1_e_gate), bf(l1_e_up), bf(l1_e_down))
    out = _combine(pos_tiles, x3, y, meta, vec(final_norm_g))
    return out.reshape(batch, seq, d)
```
